```python
import math
import jax, jax.numpy as jnp
from jax import lax
import numpy as np

D_MODEL = 2048
BATCH = 2
SEQ = 4096
DEPTH = 4
DEC_BATCH = 8
DEC_SEQ = 4
PAST_LEN = 16384
PAGE_SIZE = 128

N_A_LAYERS = DEPTH // 2
N_B_LAYERS = DEPTH - N_A_LAYERS
CONV_WIDTH = 3
CONV_DIM = D_MODEL
N_HEADS = 8
HEAD_DIM = D_MODEL // (2 * N_HEADS)
V_DIM = 2 * HEAD_DIM
QK_DIM = N_HEADS * 2 * HEAD_DIM
ATTN_DIM = N_HEADS * V_DIM
ROT_DIM = HEAD_DIM // 4
ROPE_THETA = 500000.0
Q_BLOCK = 128
EPS = 1e-6
LAMBDA_INITS = tuple(0.8 - 0.6 * math.exp(-0.3 * (N_A_LAYERS + j)) for j in range(N_B_LAYERS))

kernel_name = 'yoco_shortconv_diffattn_step'


def rms_norm(x, g):
    xf = x.astype(jnp.float32)
    y = xf * lax.rsqrt(jnp.mean(xf * xf, axis=-1, keepdims=True) + EPS)
    return (y * g.astype(jnp.float32)).astype(x.dtype)


def rope(x, pos):
    inv = ROPE_THETA ** (-jnp.arange(0, ROT_DIM, 2, dtype=jnp.float32) / ROT_DIM)
    ang = pos.astype(jnp.float32)[:, None] * inv[None, :]
    cos = jnp.cos(ang)[:, None, None, :]
    sin = jnp.sin(ang)[:, None, None, :]
    xr = x[..., :ROT_DIM].astype(jnp.float32)
    x1, x2 = xr[..., :ROT_DIM // 2], xr[..., ROT_DIM // 2:]
    rot = jnp.concatenate([x1 * cos - x2 * sin, x2 * cos + x1 * sin], axis=-1).astype(x.dtype)
    return jnp.concatenate([rot, x[..., ROT_DIM:]], axis=-1)


def short_conv_layer(x, prev_u, norm_g, w_in, conv_w, w_out):
    h = rms_norm(x, norm_g) @ w_in
    b_gate, c_gate, hx, z = jnp.split(h, 4, axis=-1)
    u = c_gate * hx
    upad = jnp.concatenate([prev_u.astype(u.dtype), u], axis=1)
    conv = lax.conv_general_dilated(upad, conv_w[:, None, :].astype(u.dtype), window_strides=(1,),
                                    padding='VALID', dimension_numbers=('NWC', 'WIO', 'NWC'),
                                    feature_group_count=CONV_DIM)
    y = (b_gate * conv * jax.nn.silu(z)) @ w_out
    return x + y, upad[:, -(CONV_WIDTH - 1):]


def shared_kv(x, pos, norm_kv, w_kv, k_norm):
    B, T, _ = x.shape
    kv = rms_norm(x, norm_kv) @ w_kv
    k, v = jnp.split(kv, [QK_DIM], axis=-1)
    k = rope(rms_norm(k.reshape(B, T, N_HEADS, 2, HEAD_DIM), k_norm), pos)
    return k, v.reshape(B, T, N_HEADS, V_DIM)


def diff_query(x, pos, norm_g, w_in, q_norm):
    B, T, _ = x.shape
    h = rms_norm(x, norm_g) @ w_in
    q, z = jnp.split(h, [QK_DIM], axis=-1)
    q = rope(rms_norm(q.reshape(B, T, N_HEADS, 2, HEAD_DIM), q_norm), pos)
    return q, z


def diff_lambda(lam, lam_init):
    lf = lam.astype(jnp.float32)
    return jnp.exp(jnp.sum(lf[0] * lf[1])) - jnp.exp(jnp.sum(lf[2] * lf[3])) + lam_init


def diff_weights(s, mask, lam_full):
    p = jax.nn.softmax(jnp.where(mask, s, -jnp.inf), axis=-1)
    return p[:, :, 0] - lam_full * p[:, :, 1]


def diff_attn_prompt(q, k, v, lam_full):
    B, S = q.shape[:2]
    nb = S // Q_BLOCK
    qb = q.reshape(B, nb, Q_BLOCK, N_HEADS, 2, HEAD_DIM).swapaxes(0, 1)
    kpos = jnp.arange(S)
    scale = HEAD_DIM ** -0.5

    def block(args):
        q_blk, i = args
        qpos = i * Q_BLOCK + jnp.arange(Q_BLOCK)
        s = jnp.einsum('bqhcd,bkhcd->bhcqk', q_blk, k, preferred_element_type=jnp.float32) * scale
        w = diff_weights(s, kpos[None, :] <= qpos[:, None], lam_full)
        return jnp.einsum('bhqk,bkhe->bqhe', w.astype(v.dtype), v)

    o = lax.map(block, (qb, jnp.arange(nb)))
    return o.swapaxes(0, 1).reshape(B, S, N_HEADS, V_DIM)


def diff_attn_sample(q, k_new, v_new, k_past, v_past, lam_full):
    T = q.shape[1]
    P = k_past.shape[1]
    scale = HEAD_DIM ** -0.5
    s_past = jnp.einsum('bqhcd,bkhcd->bhcqk', q, k_past, preferred_element_type=jnp.float32)
    s_new = jnp.einsum('bqhcd,bkhcd->bhcqk', q, k_new, preferred_element_type=jnp.float32)
    s = jnp.concatenate([s_past, s_new], axis=-1) * scale
    kpos = jnp.arange(P + T)
    qpos = P + jnp.arange(T)
    w = diff_weights(s, kpos[None, :] <= qpos[:, None], lam_full).astype(v_new.dtype)
    return (jnp.einsum('bhqk,bkhe->bqhe', w[..., :P], v_past)
            + jnp.einsum('bhqk,bkhe->bqhe', w[..., P:], v_new))


def diff_output(o, z, subln_g, lam_init, w_out):
    B, T = o.shape[:2]
    o = rms_norm(o, subln_g) * (1.0 - lam_init)
    return (o.reshape(B, T, ATTN_DIM) * jax.nn.silu(z)) @ w_out


def setup_inputs(seed: int = 0) -> dict:
    key = jax.random.key(seed)
    ks = jax.random.split(key, 20)
    f32 = jnp.float32
    n_pages = PAST_LEN // PAGE_SIZE
    n_used = DEC_BATCH * n_pages
    n_phys = n_used + max(1, n_used // 4)

    def nrm(k, shape, scale):
        return jax.random.normal(k, shape, f32) * scale

    def gain(k, shape):
        return 1.0 + 0.02 * jax.random.normal(k, shape, f32)

    page_table = jax.random.permutation(ks[0], n_phys)[:n_used].reshape(DEC_BATCH, n_pages).astype(jnp.int32)
    return {
        'x_prompt': nrm(ks[1], (BATCH, SEQ, D_MODEL), 1.0),
        'x_sample': nrm(ks[2], (DEC_BATCH, DEC_SEQ, D_MODEL), 1.0),
        'state_conv': nrm(ks[3], (N_A_LAYERS, DEC_BATCH, CONV_WIDTH - 1, CONV_DIM), 1.0),
        'cache_k': nrm(ks[4], (n_phys, PAGE_SIZE, N_HEADS, 2, HEAD_DIM), 1.0),
        'cache_v': nrm(ks[5], (n_phys, PAGE_SIZE, N_HEADS, V_DIM), 1.0),
        'page_table': page_table,
        'norm_a': gain(ks[6], (N_A_LAYERS, D_MODEL)),
        'w_in_a': nrm(ks[7], (N_A_LAYERS, D_MODEL, 4 * CONV_DIM), D_MODEL ** -0.5),
        'conv_w': nrm(ks[8], (N_A_LAYERS, CONV_WIDTH, CONV_DIM), CONV_WIDTH ** -0.5),
        'w_out_a': nrm(ks[9], (N_A_LAYERS, CONV_DIM, D_MODEL), CONV_DIM ** -0.5),
        'norm_kv': gain(ks[10], (D_MODEL,)),
        'w_kv': nrm(ks[11], (D_MODEL, QK_DIM + ATTN_DIM), D_MODEL ** -0.5),
        'k_norm': gain(ks[12], (HEAD_DIM,)),
        'norm_b': gain(ks[13], (N_B_LAYERS, D_MODEL)),
        'w_in_b': nrm(ks[14], (N_B_LAYERS, D_MODEL, QK_DIM + ATTN_DIM), D_MODEL ** -0.5),
        'q_norm': gain(ks[15], (N_B_LAYERS, HEAD_DIM)),
        'lam': nrm(ks[16], (N_B_LAYERS, 4, HEAD_DIM), 0.1),
        'subln_w': gain(ks[17], (N_B_LAYERS, V_DIM)),
        'w_out_b': nrm(ks[18], (N_B_LAYERS, ATTN_DIM, D_MODEL), ATTN_DIM ** -0.5),
    }


def reference(x_prompt, x_sample, state_conv, cache_k, cache_v, page_table,
              norm_a, w_in_a, conv_w, w_out_a, norm_kv, w_kv, k_norm,
              norm_b, w_in_b, q_norm, lam, subln_w, w_out_b):
    n_seq, n_pages = page_table.shape
    past = n_pages * PAGE_SIZE
    pos_p = jnp.arange(x_prompt.shape[1])
    pos_s = past + jnp.arange(x_sample.shape[1])
    k_past = cache_k[page_table].reshape(n_seq, past, N_HEADS, 2, HEAD_DIM)
    v_past = cache_v[page_table].reshape(n_seq, past, N_HEADS, V_DIM)

    xp, xs = x_prompt, x_sample
    conv_p, conv_s = [], []
    k_p = v_p = k_s = v_s = None
    for layer in range(DEPTH):
        if layer < N_A_LAYERS:
            zero_u = jnp.zeros((xp.shape[0], CONV_WIDTH - 1, CONV_DIM), xp.dtype)
            xp, cp = short_conv_layer(xp, zero_u, norm_a[layer], w_in_a[layer], conv_w[layer], w_out_a[layer])
            xs, cs = short_conv_layer(xs, state_conv[layer], norm_a[layer], w_in_a[layer], conv_w[layer], w_out_a[layer])
            conv_p.append(cp)
            conv_s.append(cs)
            if layer == N_A_LAYERS - 1:
                k_p, v_p = shared_kv(xp, pos_p, norm_kv, w_kv, k_norm)
                k_s, v_s = shared_kv(xs, pos_s, norm_kv, w_kv, k_norm)
        else:
            j = layer - N_A_LAYERS
            lam_init = LAMBDA_INITS[j]
            lam_full = diff_lambda(lam[j], lam_init)
            q, z = diff_query(xp, pos_p, norm_b[j], w_in_b[j], q_norm[j])
            xp = xp + diff_output(diff_attn_prompt(q, k_p, v_p, lam_full), z, subln_w[j], lam_init, w_out_b[j])
            q, z = diff_query(xs, pos_s, norm_b[j], w_in_b[j], q_norm[j])
            xs = xs + diff_output(diff_attn_sample(q, k_s, v_s, k_past, v_past, lam_full), z, subln_w[j], lam_init, w_out_b[j])

    new_conv_p = jnp.stack(conv_p)
    new_conv_s = jnp.stack(conv_s)
    return (xp, xs, k_p, v_p, new_conv_p, k_s, v_s, new_conv_s)
```

```python
import functools
import math

import jax
import jax.numpy as jnp
from jax import lax
from jax.experimental import pallas as pl
from jax.experimental.pallas import tpu as pltpu

EPS = 1e-6
ROPE_THETA = 500000.0
LANES = 128
SUBLANES = 8
VMEM_LIMIT_BYTES = 56 * 1024 * 1024

_F32 = jnp.float32
_BF16 = jnp.bfloat16


def _dot(a, b):
    return jnp.dot(a, b, preferred_element_type=_F32)


def _dot_nt(a, b):
    return lax.dot_general(a, b, (((1,), (1,)), ((), ())), preferred_element_type=_F32)


def _silu(z):
    return z * (1.0 / (1.0 + jnp.exp(-z)))


def _rms_scale(x):
    return lax.rsqrt(jnp.mean(x * x, axis=-1, keepdims=True) + EPS)


def _params(*sem):
    return pltpu.CompilerParams(dimension_semantics=sem, vmem_limit_bytes=VMEM_LIMIT_BYTES)


def _conv_layer_kernel(*refs, prompt, tiles_per_seq, seq_rows):
    if prompt:
        (x_ref, g_ref, wb_ref, wc_ref, wh_ref, wz_ref, cw_ref, wo_ref,
         y_ref, u_ref, xn_scr, acc_scr, carry_scr) = refs
    else:
        (x_ref, g_ref, wb_ref, wc_ref, wh_ref, wz_ref, cw_ref, wo_ref, p1_ref, p2_ref,
         y_ref, u_ref, xn_scr, acc_scr) = refs
    i = pl.program_id(0)
    j = pl.program_id(1)
    nj = pl.num_programs(1)

    @pl.when(j == 0)
    def _():
        x = x_ref[...]
        xn_scr[...] = (x * _rms_scale(x) * g_ref[...]).astype(_BF16)

    xn = xn_scr[...]
    b = _dot(xn, wb_ref[...])
    c = _dot(xn, wc_ref[...])
    h = _dot(xn, wh_ref[...])
    z = _dot(xn, wz_ref[...])
    u = c * h
    tm = u.shape[0]
    row = lax.broadcasted_iota(jnp.int32, u.shape, 0)
    if prompt:
        first = (i % tiles_per_seq) == 0
        cr = carry_scr[j]
        c0 = jnp.where(first, 0.0, cr[SUBLANES - 2:SUBLANES - 1, :])
        c1 = jnp.where(first, 0.0, cr[SUBLANES - 1:SUBLANES, :])
        um1 = jnp.where(row == 0, c1, pltpu.roll(u, 1, 0))
        um2 = jnp.where(row == 0, c0, jnp.where(row == 1, c1, pltpu.roll(u, 2, 0)))
        tail = u[tm - SUBLANES:tm, :]
        carry_scr[j] = tail
        u_ref[0] = tail
    else:
        t = row % seq_rows
        um1 = jnp.where(t == 0, p1_ref[...], pltpu.roll(u, 1, 0))
        um2 = jnp.where(t <= 1, p2_ref[...], pltpu.roll(u, 2, 0))
        u_ref[...] = u
    conv = cw_ref[0:1, :] * um2 + cw_ref[1:2, :] * um1 + cw_ref[2:3, :] * u
    gated = (b * conv * _silu(z)).astype(_BF16)
    contrib = _dot(gated, wo_ref[...])

    @pl.when(j == 0)
    def _():
        acc_scr[...] = contrib

    @pl.when(j > 0)
    def _():
        acc_scr[...] += contrib

    @pl.when(j == nj - 1)
    def _():
        y_ref[...] = x_ref[...] + acc_scr[...]


def _conv_layer(x, norm_g, w_in, conv_w, w_out, *, seq_rows, prev=None, tm, tc):
    m, d = x.shape
    c = conv_w.shape[1]
    prompt = prev is None
    tm = min(tm, m)
    tc = min(tc, c)
    ni, nj = m // tm, c // tc
    tiles_per_seq = max(seq_rows // tm, 1)
    n_seq = m // seq_rows

    def w_in_spec(gate):
        return pl.BlockSpec((d, tc), lambda i, j, gate=gate: (0, gate * nj + j))

    in_specs = [
        pl.BlockSpec((tm, d), lambda i, j: (i, 0)),
        pl.BlockSpec((1, d), lambda i, j: (0, 0)),
        w_in_spec(0), w_in_spec(1), w_in_spec(2), w_in_spec(3),
        pl.BlockSpec((3, tc), lambda i, j: (0, j)),
        pl.BlockSpec((tc, d), lambda i, j: (j, 0)),
    ]
    args = [x, norm_g.reshape(1, d), w_in, w_in, w_in, w_in, conv_w, w_out]
    scratch = [pltpu.VMEM((tm, d), _BF16), pltpu.VMEM((tm, d), _F32)]
    if prompt:
        u_shape = jax.ShapeDtypeStruct((n_seq, SUBLANES, c), _F32)
        u_spec = pl.BlockSpec((1, SUBLANES, tc), lambda i, j: (i // tiles_per_seq, 0, j))
        scratch.append(pltpu.VMEM((nj, SUBLANES, tc), _F32))
    else:
        in_specs += [pl.BlockSpec((tm, tc), lambda i, j: (i, j))] * 2
        args += list(prev)
        u_shape = jax.ShapeDtypeStruct((m, c), _F32)
        u_spec = pl.BlockSpec((tm, tc), lambda i, j: (i, j))
    return pl.pallas_call(
        functools.partial(_conv_layer_kernel, prompt=prompt, tiles_per_seq=tiles_per_seq,
                          seq_rows=seq_rows),
        grid=(ni, nj),
        in_specs=in_specs,
        out_specs=[pl.BlockSpec((tm, d), lambda i, j: (i, 0)), u_spec],
        out_shape=[jax.ShapeDtypeStruct((m, d), _F32), u_shape],
        scratch_shapes=scratch,
        compiler_params=_params("arbitrary", "arbitrary"),
        name="conv_layer_prompt" if prompt else "conv_layer_sample",
    )(*args)


def _proj_kernel(*refs, nja, out_kinds):
    x_ref, g_ref, w_ref, hg_ref, cos_ref, sa_ref, sb_ref = refs[:7]
    outs = dict(zip(out_kinds, refs[7:7 + len(out_kinds)]))
    xn_scr = refs[7 + len(out_kinds)]
    j = pl.program_id(1)

    @pl.when(j == 0)
    def _():
        x = x_ref[...]
        xn_scr[...] = (x * _rms_scale(x) * g_ref[...]).astype(_BF16)

    h = _dot(xn_scr[...], w_ref[...])

    @pl.when(j < nja)
    def _():
        cos, sa, sb, hg = cos_ref[...], sa_ref[...], sb_ref[...], hg_ref[...]
        for grp in range(h.shape[1] // LANES):
            sl = slice(grp * LANES, (grp + 1) * LANES)
            xg = h[:, sl]
            yn = xg * _rms_scale(xg) * hg
            y = (yn * cos + pltpu.roll(yn, LANES - 16, 1) * sa + pltpu.roll(yn, 16, 1) * sb)
            if "a32" in outs:
                outs["a32"][:, sl] = y
            if "a16" in outs:
                outs["a16"][:, sl] = y.astype(_BF16)

    @pl.when(j >= nja)
    def _():
        if "b32" in outs:
            outs["b32"][...] = h
        if "b16" in outs:
            outs["b16"][...] = h.astype(_BF16)


def _proj(x, norm_g, w, head_gain, tables, *, seq_rows, out_kinds, tm, tn, name):
    m, d = x.shape
    n = w.shape[1] // 2
    tm = min(tm, m)
    tn = min(tn, n)
    ni, nja = m // tm, n // tn
    tiles_per_seq = max(seq_rows // tm, 1)
    tab_spec = pl.BlockSpec((tm, LANES), lambda i, j: (i % tiles_per_seq, 0))
    a_spec = pl.BlockSpec((tm, tn), lambda i, j: (i, jnp.minimum(j, nja - 1)))
    b_spec = pl.BlockSpec((tm, tn), lambda i, j: (i, jnp.maximum(j - nja, 0)))
    out_specs, out_shape = [], []
    for kind in out_kinds:
        out_specs.append(a_spec if kind[0] == "a" else b_spec)
        out_shape.append(jax.ShapeDtypeStruct((m, n), _F32 if kind.endswith("32") else _BF16))
    return pl.pallas_call(
        functools.partial(_proj_kernel, nja=nja, out_kinds=tuple(out_kinds)),
        grid=(ni, 2 * nja),
        in_specs=[
            pl.BlockSpec((tm, d), lambda i, j: (i, 0)),
            pl.BlockSpec((1, d), lambda i, j: (0, 0)),
            pl.BlockSpec((d, tn), lambda i, j: (0, j)),
            pl.BlockSpec((1, LANES), lambda i, j: (0, 0)),
            tab_spec, tab_spec, tab_spec,
        ],
        out_specs=out_specs,
        out_shape=out_shape,
        scratch_shapes=[pltpu.VMEM((tm, d), _BF16)],
        compiler_params=_params("arbitrary", "arbitrary"),
        name=name,
    )(x, norm_g.reshape(1, d), w, head_gain.reshape(1, LANES), *tables)


def _lambda_full(lam_ref, lam_init):
    lf = lam_ref[...]
    t1 = jnp.sum(lf[0:1, :] * lf[1:2, :], axis=-1, keepdims=True)
    t2 = jnp.sum(lf[2:3, :] * lf[3:4, :], axis=-1, keepdims=True)
    return jnp.exp(t1) - jnp.exp(t2) + lam_init


def _softmax_update(s, v, m_ref, l_ref, acc_ref, scale, p_dtype):
    m_prev = m_ref[...]
    m_new = jnp.maximum(m_prev, jnp.max(s, axis=-1, keepdims=True))
    alpha = jnp.exp((m_prev - m_new) * scale)
    p = jnp.exp((s - m_new) * scale)
    l_ref[...] = alpha * l_ref[...] + jnp.sum(p, axis=-1, keepdims=True)
    acc_ref[...] = alpha * acc_ref[...] + _dot(p.astype(p_dtype), v)
    m_ref[...] = m_new


def _head_output(o1, o2, lam_full, z, subln, lam_init):
    o = o1 - lam_full * o2
    on = o * _rms_scale(o) * subln * (1.0 - lam_init)
    return on * _silu(z)


def _attn_prompt_kernel(lam_ref, q_ref, k_ref, v_ref, z_ref, sg_ref, o_ref,
                        m_scr, l_scr, acc_scr, *, tq, hd, scale, lam_init):
    qi = pl.program_id(2)
    q = q_ref[0]
    m_scr[...] = jnp.full(m_scr.shape, -jnp.inf, _F32)
    l_scr[...] = jnp.zeros(l_scr.shape, _F32)
    acc_scr[...] = jnp.zeros(acc_scr.shape, _F32)

    def step(kt, masked):
        start = pl.multiple_of(kt * tq, tq)
        ks = k_ref[0, pl.ds(start, tq), :]
        vs = v_ref[0, pl.ds(start, tq), :]
        for c in range(2):
            s = _dot_nt(q[:, c * hd:(c + 1) * hd], ks[:, c * hd:(c + 1) * hd])
            if masked:
                row = lax.broadcasted_iota(jnp.int32, s.shape, 0)
                col = lax.broadcasted_iota(jnp.int32, s.shape, 1)
                s = jnp.where(col <= row, s, -jnp.inf)
            _softmax_update(s, vs, m_scr.at[c], l_scr.at[c], acc_scr.at[c], scale, _BF16)

    def body(kt, carry):
        step(kt, False)
        return carry

    lax.fori_loop(0, qi, body, 0)
    step(qi, True)

    o1 = acc_scr[0] * (1.0 / l_scr[0])
    o2 = acc_scr[1] * (1.0 / l_scr[1])
    out = _head_output(o1, o2, _lambda_full(lam_ref, lam_init), z_ref[0].astype(_F32),
                       sg_ref[...], lam_init)
    o_ref[0] = out.astype(o_ref.dtype)


def _attn_prompt(q, k, v, z, lam, subln, lam_init, *, hd, tq):
    b, s, width = q.shape
    vd = 2 * hd
    n_heads = width // vd
    tq = min(tq, s)
    q_spec = pl.BlockSpec((1, tq, vd), lambda bi, h, qi: (bi, qi, h))
    kv_spec = pl.BlockSpec((1, s, vd), lambda bi, h, qi: (bi, 0, h))
    return pl.pallas_call(
        functools.partial(_attn_prompt_kernel, tq=tq, hd=hd, scale=hd ** -0.5, lam_init=lam_init),
        grid=(b, n_heads, s // tq),
        in_specs=[
            pl.BlockSpec((4, hd), lambda bi, h, qi: (0, 0)),
            q_spec, kv_spec, kv_spec, q_spec,
            pl.BlockSpec((1, vd), lambda bi, h, qi: (0, 0)),
        ],
        out_specs=q_spec,
        out_shape=jax.ShapeDtypeStruct((b, s, width), _BF16),
        scratch_shapes=[pltpu.VMEM((2, tq, 1), _F32), pltpu.VMEM((2, tq, 1), _F32),
                        pltpu.VMEM((2, tq, vd), _F32)],
        compiler_params=_params("arbitrary", "arbitrary", "arbitrary"),
        name="attn_prompt",
    )(lam, q, k, v, z, subln.reshape(1, vd))


def _attn_decode_kernel(pt_ref, lam_ref, q_ref, *refs, pages_per_step, n_heads, hd, t_real, tp,
                        scale, lam_init):
    del pt_ref
    k_refs = refs[:pages_per_step]
    v_refs = refs[pages_per_step:2 * pages_per_step]
    kn_ref, vn_ref, z_ref, sg_ref, o_ref, m_scr, l_scr, acc_scr = refs[2 * pages_per_step:]
    step = pl.program_id(1)
    vd = 2 * hd

    @pl.when(step == 0)
    def _():
        m_scr[...] = jnp.full(m_scr.shape, -jnp.inf, _F32)
        l_scr[...] = jnp.zeros(l_scr.shape, _F32)
        acc_scr[...] = jnp.zeros(acc_scr.shape, _F32)

    q = q_ref[0]
    kblk = jnp.concatenate([r[0] for r in k_refs], axis=0)
    vblk = jnp.concatenate([r[0] for r in v_refs], axis=0)
    _softmax_update(_dot_nt(q, kblk), vblk, m_scr, l_scr, acc_scr, scale, _F32)

    @pl.when(step == pl.num_programs(1) - 1)
    def _():
        s = _dot_nt(q, kn_ref[0])
        t = lax.broadcasted_iota(jnp.int32, s.shape, 0) % tp
        jj = lax.broadcasted_iota(jnp.int32, s.shape, 1)
        s = jnp.where((jj <= t) & (jj < t_real), s, -jnp.inf)
        _softmax_update(s, vn_ref[0], m_scr, l_scr, acc_scr, scale, _F32)
        lam_full = _lambda_full(lam_ref, lam_init)
        acc = acc_scr[...] * (1.0 / l_scr[...])
        for h in range(n_heads):
            r1 = (2 * h) * tp
            cols = slice(h * vd, (h + 1) * vd)
            out = _head_output(acc[r1:r1 + tp, cols], acc[r1 + tp:r1 + 2 * tp, cols], lam_full,
                               z_ref[0, :, cols], sg_ref[...], lam_init)
            o_ref[0, :, cols] = out.astype(o_ref.dtype)


def _attn_decode(qbd, cache_k, cache_v, page_table, k_new, v_new, z, lam, subln, lam_init, *,
                 hd, t_real, pages_per_step):
    db, rows, width = qbd.shape
    page = cache_k.shape[1]
    n_pages = page_table.shape[1]
    vd = 2 * hd
    n_heads = width // vd
    tp = rows // (2 * n_heads)
    g = math.gcd(pages_per_step, n_pages)

    def page_spec(slot):
        return pl.BlockSpec((1, page, width),
                            lambda b, s, pt, slot=slot: (pt[b * n_pages + s * g + slot], 0, 0))

    row_spec = pl.BlockSpec((1, tp, width), lambda b, s, pt: (b, 0, 0))
    grid_spec = pltpu.PrefetchScalarGridSpec(
        num_scalar_prefetch=1,
        grid=(db, n_pages // g),
        in_specs=[
            pl.BlockSpec((4, hd), lambda b, s, pt: (0, 0)),
            pl.BlockSpec((1, rows, width), lambda b, s, pt: (b, 0, 0)),
            *[page_spec(slot) for slot in range(g)],
            *[page_spec(slot) for slot in range(g)],
            row_spec, row_spec, row_spec,
            pl.BlockSpec((1, vd), lambda b, s, pt: (0, 0)),
        ],
        out_specs=row_spec,
        scratch_shapes=[pltpu.VMEM((rows, 1), _F32), pltpu.VMEM((rows, 1), _F32),
                        pltpu.VMEM((rows, width), _F32)],
    )
    return pl.pallas_call(
        functools.partial(_attn_decode_kernel, pages_per_step=g, n_heads=n_heads, hd=hd,
                          t_real=t_real, tp=tp, scale=hd ** -0.5, lam_init=lam_init),
        grid_spec=grid_spec,
        out_shape=jax.ShapeDtypeStruct((db, tp, width), _BF16),
        compiler_params=_params("arbitrary", "arbitrary"),
        name="attn_decode",
    )(page_table.reshape(-1), lam, qbd, *([cache_k] * g), *([cache_v] * g), k_new, v_new, z,
      subln.reshape(1, vd))


def _out_proj_kernel(x_ref, g_ref, w_ref, y_ref):
    y_ref[...] = x_ref[...] + _dot(g_ref[...], w_ref[...])


def _out_proj(x, g, w, *, tm):
    m, d = x.shape
    a = g.shape[1]
    tm = min(tm, m)
    return pl.pallas_call(
        _out_proj_kernel,
        grid=(m // tm,),
        in_specs=[pl.BlockSpec((tm, d), lambda i: (i, 0)),
                  pl.BlockSpec((tm, a), lambda i: (i, 0)),
                  pl.BlockSpec((a, d), lambda i: (0, 0))],
        out_specs=pl.BlockSpec((tm, d), lambda i: (i, 0)),
        out_shape=jax.ShapeDtypeStruct((m, d), _F32),
        compiler_params=_params("arbitrary"),
        name="out_proj",
    )(x, g, w)


def _rope_tables(pos, hd):
    rot = hd // 4
    half = rot // 2
    inv = ROPE_THETA ** (-jnp.arange(0, rot, 2, dtype=_F32) / rot)
    ang = pos.astype(_F32)[:, None] * inv[None, :]
    cos, sin = jnp.cos(ang), jnp.sin(ang)
    n = pos.shape[0]
    cos_t = jnp.concatenate([cos, cos, jnp.ones((n, hd - rot), _F32)], axis=1)
    sa_t = jnp.concatenate([-sin, jnp.zeros((n, hd - half), _F32)], axis=1)
    sb_t = jnp.concatenate([jnp.zeros((n, half), _F32), sin, jnp.zeros((n, hd - rot), _F32)], axis=1)
    return cos_t, sa_t, sb_t


def _pad_rows(x, tp):
    return jnp.pad(x, ((0, 0), (0, tp - x.shape[1]), (0, 0)))


def _block_diag_queries(q, n_groups, hd, tp):
    db, t, _ = q.shape
    qg = _pad_rows(q, tp).reshape(db, tp, n_groups, hd).transpose(0, 2, 1, 3)
    eye = jnp.eye(n_groups, dtype=q.dtype)
    qbd = qg[:, :, :, None, :] * eye[None, :, None, :, None]
    return qbd.reshape(db, n_groups * tp, n_groups * hd)


def kernel(x_prompt, x_sample, state_conv, cache_k, cache_v, page_table, norm_a, w_in_a, conv_w, w_out_a, norm_kv, w_kv, k_norm, norm_b, w_in_b, q_norm, lam, subln_w, w_out_b):
    bsz, seq, d = x_prompt.shape
    db, t_dec, _ = x_sample.shape
    n_a, n_b = norm_a.shape[0], norm_b.shape[0]
    n_phys, page, n_heads, _, hd = cache_k.shape
    n_pages = page_table.shape[1]
    past = n_pages * page
    width = n_heads * 2 * hd
    tp = -(-t_dec // SUBLANES) * SUBLANES

    w_in_a16, w_out_a16 = w_in_a.astype(_BF16), w_out_a.astype(_BF16)
    w_kv16, w_in_b16, w_out_b16 = w_kv.astype(_BF16), w_in_b.astype(_BF16), w_out_b.astype(_BF16)
    tab_p = _rope_tables(jnp.arange(seq), hd)
    tab_s = _rope_tables(jnp.tile(past + jnp.arange(t_dec), db), hd)
    cache_k2 = cache_k.reshape(n_phys, page, width)
    cache_v2 = cache_v.reshape(n_phys, page, width)

    xp = x_prompt.reshape(bsz * seq, d)
    xs = x_sample.reshape(db * t_dec, d)
    conv_p, conv_s = [], []
    for layer in range(n_a):
        xp, up = _conv_layer(xp, norm_a[layer], w_in_a16[layer], conv_w[layer], w_out_a16[layer],
                             seq_rows=seq, tm=512, tc=512)
        conv_p.append(up[:, SUBLANES - 2:, :])
        prev = state_conv[layer]
        zero = jnp.zeros_like(prev[:, :1])
        p1 = jnp.concatenate([prev[:, 1:2]] + [zero] * (t_dec - 1), axis=1)
        p2 = jnp.concatenate([prev[:, 0:1], prev[:, 1:2]] + [zero] * (t_dec - 2), axis=1)
        c_dim = prev.shape[-1]
        xs, us = _conv_layer(xs, norm_a[layer], w_in_a16[layer], conv_w[layer], w_out_a16[layer],
                             seq_rows=t_dec, prev=(p1.reshape(db * t_dec, c_dim), p2.reshape(db * t_dec, c_dim)),
                             tm=512, tc=512)
        conv_s.append(us.reshape(db, t_dec, c_dim)[:, t_dec - 2:, :])

    kp32, kp16, vp32, vp16 = _proj(xp, norm_kv, w_kv16, k_norm, tab_p, seq_rows=seq,
                                   out_kinds=("a32", "a16", "b32", "b16"), tm=512, tn=1024, name="kv_prompt")
    ks32, vs32 = _proj(xs, norm_kv, w_kv16, k_norm, tab_s, seq_rows=db * t_dec,
                       out_kinds=("a32", "b32"), tm=512, tn=1024, name="kv_sample")
    k_new = _pad_rows(ks32.reshape(db, t_dec, width), tp)
    v_new = _pad_rows(vs32.reshape(db, t_dec, width), tp)

    for j in range(n_b):
        lam_init = 0.8 - 0.6 * math.exp(-0.3 * (n_a + j))
        q16, z16 = _proj(xp, norm_b[j], w_in_b16[j], q_norm[j], tab_p, seq_rows=seq,
                         out_kinds=("a16", "b16"), tm=512, tn=1024, name="qz_prompt")
        gp = _attn_prompt(q16.reshape(bsz, seq, width), kp16.reshape(bsz, seq, width),
                          vp16.reshape(bsz, seq, width), z16.reshape(bsz, seq, width),
                          lam[j], subln_w[j], lam_init, hd=hd, tq=512)
        xp = _out_proj(xp, gp.reshape(bsz * seq, width), w_out_b16[j], tm=512)

        qs32, zs32 = _proj(xs, norm_b[j], w_in_b16[j], q_norm[j], tab_s, seq_rows=db * t_dec,
                           out_kinds=("a32", "b32"), tm=512, tn=1024, name="qz_sample")
        qbd = _block_diag_queries(qs32.reshape(db, t_dec, width), 2 * n_heads, hd, tp)
        gs = _attn_decode(qbd, cache_k2, cache_v2, page_table, k_new, v_new,
                          _pad_rows(zs32.reshape(db, t_dec, width), tp), lam[j], subln_w[j], lam_init,
                          hd=hd, t_real=t_dec, pages_per_step=4)
        xs = _out_proj(xs, gs[:, :t_dec].reshape(db * t_dec, width), w_out_b16[j], tm=512)

    return (xp.reshape(bsz, seq, d), xs.reshape(db, t_dec, d),
            kp32.reshape(bsz, seq, n_heads, 2, hd), vp32.reshape(bsz, seq, n_heads, 2 * hd),
            jnp.stack(conv_p),
            ks32.reshape(db, t_dec, n_heads, 2, hd), vs32.reshape(db, t_dec, n_heads, 2 * hd),
            jnp.stack(conv_s))
```

```python
import functools
import math

import jax
import jax.numpy as jnp
from jax import lax
from jax.experimental import pallas as pl
from jax.experimental.pallas import tpu as pltpu

EPS = 1e-6
ROPE_THETA = 500000.0
LANES = 128
SUBLANES = 8
MXU_WIDTH = 256
VMEM_LIMIT_BYTES = 56 * 1024 * 1024
LOG2_E = math.log2(math.e)

_F32 = jnp.float32
_BF16 = jnp.bfloat16


def _dot(a, b):
    return jnp.dot(a, b, preferred_element_type=_F32)


def _dot_nt(a, b):
    return lax.dot_general(a, b, (((1,), (1,)), ((), ())), preferred_element_type=_F32)


def _silu(z):
    return z * (1.0 / (1.0 + jnp.exp(-z)))


def _rms_scale(x):
    return lax.rsqrt(jnp.mean(x * x, axis=-1, keepdims=True) + EPS)


def _params(*sem):
    return pltpu.CompilerParams(dimension_semantics=sem, vmem_limit_bytes=VMEM_LIMIT_BYTES)


def _conv_layer_kernel(*refs, prompt, tiles_per_seq, seq_rows):
    if prompt:
        (x_ref, g_ref, wb_ref, wc_ref, wh_ref, wz_ref, cw_ref, wo_ref,
         y_ref, u_ref, xn_scr, acc_scr, carry_scr) = refs
    else:
        (x_ref, g_ref, wb_ref, wc_ref, wh_ref, wz_ref, cw_ref, wo_ref, p1_ref, p2_ref,
         y_ref, u_ref, xn_scr, acc_scr) = refs
    i = pl.program_id(0)
    j = pl.program_id(1)
    nj = pl.num_programs(1)

    @pl.when(j == 0)
    def _():
        x = x_ref[...]
        xn_scr[...] = (x * _rms_scale(x) * g_ref[...]).astype(_BF16)

    xn = xn_scr[...]
    b = _dot(xn, wb_ref[...])
    c = _dot(xn, wc_ref[...])
    h = _dot(xn, wh_ref[...])
    z = _dot(xn, wz_ref[...])
    u = c * h
    tm = u.shape[0]
    row = lax.broadcasted_iota(jnp.int32, u.shape, 0)
    if prompt:
        first = (i % tiles_per_seq) == 0
        cr = carry_scr[j]
        c0 = jnp.where(first, 0.0, cr[SUBLANES - 2:SUBLANES - 1, :])
        c1 = jnp.where(first, 0.0, cr[SUBLANES - 1:SUBLANES, :])
        um1 = jnp.where(row == 0, c1, pltpu.roll(u, 1, 0))
        um2 = jnp.where(row == 0, c0, jnp.where(row == 1, c1, pltpu.roll(u, 2, 0)))
        tail = u[tm - SUBLANES:tm, :]
        carry_scr[j] = tail
        u_ref[0] = tail
    else:
        t = row % seq_rows
        um1 = jnp.where(t == 0, p1_ref[...], pltpu.roll(u, 1, 0))
        um2 = jnp.where(t <= 1, p2_ref[...], pltpu.roll(u, 2, 0))
        u_ref[...] = u
    conv = cw_ref[0:1, :] * um2 + cw_ref[1:2, :] * um1 + cw_ref[2:3, :] * u
    gated = (b * conv * _silu(z)).astype(_BF16)
    contrib = _dot(gated, wo_ref[...])

    @pl.when(j == 0)
    def _():
        acc_scr[...] = contrib

    @pl.when(j > 0)
    def _():
        acc_scr[...] += contrib

    @pl.when(j == nj - 1)
    def _():
        y_ref[...] = x_ref[...] + acc_scr[...]


def _conv_layer(x, norm_g, w_in, conv_w, w_out, *, seq_rows, prev=None, tm, tc):
    m, d = x.shape
    c = conv_w.shape[1]
    prompt = prev is None
    tm = min(tm, m)
    tc = min(tc, c)
    ni, nj = m // tm, c // tc
    tiles_per_seq = max(seq_rows // tm, 1)

    def w_in_spec(gate):
        return pl.BlockSpec((d, tc), lambda i, j, gate=gate: (0, gate * nj + j))

    in_specs = [
        pl.BlockSpec((tm, d), lambda i, j: (i, 0)),
        pl.BlockSpec((1, d), lambda i, j: (0, 0)),
        w_in_spec(0), w_in_spec(1), w_in_spec(2), w_in_spec(3),
        pl.BlockSpec((3, tc), lambda i, j: (0, j)),
        pl.BlockSpec((tc, d), lambda i, j: (j, 0)),
    ]
    args = [x, norm_g.reshape(1, d), w_in, w_in, w_in, w_in, conv_w, w_out]
    scratch = [pltpu.VMEM((tm, d), _BF16), pltpu.VMEM((tm, d), _F32)]
    if prompt:
        u_shape = jax.ShapeDtypeStruct((ni, SUBLANES, c), _F32)
        u_spec = pl.BlockSpec((1, SUBLANES, tc), lambda i, j: (i, 0, j))
        scratch.append(pltpu.VMEM((nj, SUBLANES, tc), _F32))
    else:
        in_specs += [pl.BlockSpec((tm, tc), lambda i, j: (i, j))] * 2
        args += list(prev)
        u_shape = jax.ShapeDtypeStruct((m, c), _F32)
        u_spec = pl.BlockSpec((tm, tc), lambda i, j: (i, j))
    return pl.pallas_call(
        functools.partial(_conv_layer_kernel, prompt=prompt, tiles_per_seq=tiles_per_seq,
                          seq_rows=seq_rows),
        grid=(ni, nj),
        in_specs=in_specs,
        out_specs=[pl.BlockSpec((tm, d), lambda i, j: (i, 0)), u_spec],
        out_shape=[jax.ShapeDtypeStruct((m, d), _F32), u_shape],
        scratch_shapes=scratch,
        compiler_params=_params("arbitrary", "arbitrary"),
        name="conv_layer_prompt" if prompt else "conv_layer_sample",
    )(*args)


def _proj_heads_kernel(x_ref, g_ref, w_ref, hg_ref, cos_ref, sa_ref, sb_ref, *out_refs, out_kinds):
    outs = dict(zip(out_kinds, out_refs))
    x = x_ref[...]
    xn = (x * _rms_scale(x) * g_ref[...]).astype(_BF16)
    cos, sa, sb, hg = cos_ref[...], sa_ref[...], sb_ref[...], hg_ref[...]
    for chunk in range(w_ref.shape[1] // MXU_WIDTH):
        h = _dot(xn, w_ref[:, chunk * MXU_WIDTH:(chunk + 1) * MXU_WIDTH])
        for grp in range(MXU_WIDTH // LANES):
            sl = slice(chunk * MXU_WIDTH + grp * LANES, chunk * MXU_WIDTH + (grp + 1) * LANES)
            xg = h[:, grp * LANES:(grp + 1) * LANES]
            yn = xg * _rms_scale(xg) * hg
            y = yn * cos + pltpu.roll(yn, LANES - 16, 1) * sa + pltpu.roll(yn, 16, 1) * sb
            if "f32" in outs:
                outs["f32"][:, sl] = y
            if "bf16" in outs:
                outs["bf16"][:, sl] = y.astype(_BF16)


def _proj_plain_kernel(x_ref, g_ref, w_ref, *out_refs, out_kinds):
    outs = dict(zip(out_kinds, out_refs))
    x = x_ref[...]
    xn = (x * _rms_scale(x) * g_ref[...]).astype(_BF16)
    h = _dot(xn, w_ref[...])
    if "f32" in outs:
        outs["f32"][...] = h
    if "bf16" in outs:
        outs["bf16"][...] = h.astype(_BF16)


def _proj(x, norm_g, w, *, n, col_block, out_kinds, tm, name, head_gain=None, tables=None, seq_rows=None):
    m, d = x.shape
    tm = min(tm, m)
    row_spec = pl.BlockSpec((tm, n), lambda i: (i, 0))
    in_specs = [pl.BlockSpec((tm, d), lambda i: (i, 0)),
                pl.BlockSpec((1, d), lambda i: (0, 0)),
                pl.BlockSpec((d, n), lambda i: (0, col_block))]
    args = [x, norm_g.reshape(1, d), w]
    if head_gain is None:
        body = _proj_plain_kernel
    else:
        body = _proj_heads_kernel
        tiles_per_seq = max(seq_rows // tm, 1)
        tab_spec = pl.BlockSpec((tm, LANES), lambda i: (i % tiles_per_seq, 0))
        in_specs += [pl.BlockSpec((1, LANES), lambda i: (0, 0)), tab_spec, tab_spec, tab_spec]
        args += [head_gain.reshape(1, LANES), *tables]
    return pl.pallas_call(
        functools.partial(body, out_kinds=tuple(out_kinds)),
        grid=(m // tm,),
        in_specs=in_specs,
        out_specs=[row_spec] * len(out_kinds),
        out_shape=[jax.ShapeDtypeStruct((m, n), _F32 if kind == "f32" else _BF16) for kind in out_kinds],
        compiler_params=_params("arbitrary"),
        name=name,
    )(*args)


def _lambda_full(lam_ref, lam_init):
    lf = lam_ref[...]
    t1 = jnp.sum(lf[0:1, :] * lf[1:2, :], axis=-1, keepdims=True)
    t2 = jnp.sum(lf[2:3, :] * lf[3:4, :], axis=-1, keepdims=True)
    return jnp.exp(t1) - jnp.exp(t2) + lam_init


def _head_output(o1, o2, lam_full, z, subln, lam_init):
    o = o1 - lam_full * o2
    on = o * _rms_scale(o) * subln * (1.0 - lam_init)
    return on * _silu(z)


def _attn_prompt_kernel(lam_ref, q_ref, k_ref, v_ref, z_ref, sg_ref, o_ref,
                        kt_scr, m_scr, l_scr, acc_scr, *, tq, hd, lam_init):
    qi = pl.program_id(2)
    n_kt = kt_scr.shape[0]
    vd = 2 * hd

    @pl.when(qi == 0)
    def _():
        for t in range(n_kt):
            kt_scr[t] = k_ref[0, t * tq:(t + 1) * tq, :].T

    q = q_ref[0]
    m_scr[...] = jnp.full(m_scr.shape, -jnp.inf, _F32)
    l_scr[...] = jnp.zeros(l_scr.shape, _F32)
    acc_scr[...] = jnp.zeros(acc_scr.shape, _F32)

    def step(kt, masked):
        start = pl.multiple_of(kt * tq, tq)
        vs = v_ref[0, pl.ds(start, tq), :]
        for c in range(2):
            s = _dot(q[:, c * hd:(c + 1) * hd], kt_scr[kt, c * hd:(c + 1) * hd, :])
            if masked:
                row = lax.broadcasted_iota(jnp.int32, s.shape, 0)
                col = lax.broadcasted_iota(jnp.int32, s.shape, 1)
                s = jnp.where(col <= row, s, -jnp.inf)
            m_prev = m_scr[c]
            m_new = jnp.maximum(m_prev, jnp.max(s, axis=-1, keepdims=True))
            alpha = jnp.exp2(m_prev - m_new)
            p = jnp.exp2(s - pltpu.repeat(m_new, tq // LANES, 1))
            l_scr[c] = alpha * l_scr[c] + jnp.sum(p, axis=-1, keepdims=True)
            acc_scr[c] = pltpu.repeat(alpha, vd // LANES, 1) * acc_scr[c] + _dot(p.astype(_BF16), vs)
            m_scr[c] = m_new

    def body(kt, carry):
        step(kt, False)
        return carry

    lax.fori_loop(0, qi, body, 0)
    step(qi, True)

    o1 = acc_scr[0] * pltpu.repeat(1.0 / l_scr[0], vd // LANES, 1)
    o2 = acc_scr[1] * pltpu.repeat(1.0 / l_scr[1], vd // LANES, 1)
    out = _head_output(o1, o2, _lambda_full(lam_ref, lam_init), z_ref[0].astype(_F32),
                       sg_ref[...], lam_init)
    o_ref[0] = out.astype(o_ref.dtype)


def _attn_prompt(q, k, v, z, lam, subln, lam_init, *, hd, tq):
    b, s, width = q.shape
    vd = 2 * hd
    n_heads = width // vd
    tq = min(tq, s)
    q_spec = pl.BlockSpec((1, tq, vd), lambda bi, h, qi: (bi, qi, h))
    kv_spec = pl.BlockSpec((1, s, vd), lambda bi, h, qi: (bi, 0, h))
    return pl.pallas_call(
        functools.partial(_attn_prompt_kernel, tq=tq, hd=hd, lam_init=lam_init),
        grid=(b, n_heads, s // tq),
        in_specs=[
            pl.BlockSpec((4, hd), lambda bi, h, qi: (0, 0)),
            q_spec, kv_spec, kv_spec, q_spec,
            pl.BlockSpec((1, vd), lambda bi, h, qi: (0, 0)),
        ],
        out_specs=q_spec,
        out_shape=jax.ShapeDtypeStruct((b, s, width), _BF16),
        scratch_shapes=[pltpu.VMEM((s // tq, vd, tq), _BF16),
                        pltpu.VMEM((2, tq, LANES), _F32), pltpu.VMEM((2, tq, LANES), _F32),
                        pltpu.VMEM((2, tq, vd), _F32)],
        compiler_params=_params("arbitrary", "arbitrary", "arbitrary"),
        name="attn_prompt",
    )(lam, q, k, v, z, subln.reshape(1, vd))


def _attn_decode_kernel(pt_ref, lam_ref, q_ref, kn_ref, vn_ref, z_ref, sg_ref, *refs,
                        pages_per_step, n_heads, t_real, tp, scale, lam_init):
    del pt_ref
    hd = q_ref.shape[2]
    vd = 2 * hd
    v_blocks = vd // LANES
    k_rows = 2 * n_heads
    k_refs = refs[:pages_per_step]
    v_refs = refs[pages_per_step:(1 + v_blocks) * pages_per_step]
    o_ref, m_scr, l_scr, acc_scr = refs[(1 + v_blocks) * pages_per_step:]
    page = k_refs[0].shape[1] // k_rows
    step = pl.program_id(1)
    n_steps = pl.num_programs(1)

    @pl.when(step == 0)
    def _():
        m_scr[...] = jnp.full(m_scr.shape, -jnp.inf, _F32)
        l_scr[...] = jnp.zeros(l_scr.shape, _F32)
        acc_scr[...] = jnp.zeros(acc_scr.shape, _F32)

    q = q_ref[0]

    def update(k_of_group, v_of_head, mask=None):
        s = jnp.concatenate([_dot_nt(q[g * tp:(g + 1) * tp, :], k_of_group(g)) for g in range(k_rows)],
                            axis=0)
        if mask is not None:
            s = jnp.where(mask(s.shape), s, -jnp.inf)
        m_prev = m_scr[...]
        m_new = jnp.maximum(m_prev, jnp.max(s, axis=-1, keepdims=True))
        alpha = jnp.exp((m_prev - m_new) * scale)
        p = jnp.exp((s - m_new) * scale)
        l_scr[...] = alpha * l_scr[...] + jnp.sum(p, axis=-1, keepdims=True)
        pv = jnp.concatenate([_dot(p[h * 2 * tp:(h + 1) * 2 * tp, :], v_of_head(h)) for h in range(n_heads)],
                             axis=0)
        acc_scr[...] = alpha * acc_scr[...] + pv
        m_scr[...] = m_new

    def cached_k(g):
        return jnp.concatenate([r[0, pl.ds(g, page, stride=k_rows), :] for r in k_refs], axis=0)

    def cached_v(h):
        return jnp.concatenate(
            [jnp.concatenate([v_refs[v_blocks * pg + cb][0, pl.ds(h, page, stride=n_heads), :]
                              for cb in range(v_blocks)], axis=1)
             for pg in range(pages_per_step)], axis=0)

    update(cached_k, cached_v)

    @pl.when(step == n_steps - 1)
    def _():
        def new_key_mask(shape):
            t = lax.broadcasted_iota(jnp.int32, shape, 0) % tp
            jj = lax.broadcasted_iota(jnp.int32, shape, 1)
            return (jj <= t) & (jj < t_real)

        update(lambda g: kn_ref[0, :, g * hd:(g + 1) * hd], lambda h: vn_ref[0, :, h * vd:(h + 1) * vd],
               new_key_mask)
        lam_full = _lambda_full(lam_ref, lam_init)
        acc = acc_scr[...] * (1.0 / l_scr[...])
        for h in range(n_heads):
            r1 = (2 * h) * tp
            cols = slice(h * vd, (h + 1) * vd)
            out = _head_output(acc[r1:r1 + tp, :], acc[r1 + tp:r1 + 2 * tp, :], lam_full,
                               z_ref[0, :, cols], sg_ref[...], lam_init)
            o_ref[0, :, cols] = out.astype(o_ref.dtype)


def _attn_decode(q, cache_k, cache_v, page_table, k_new, v_new, z, lam, subln, lam_init, *,
                 t_real, pages_per_step):
    db, rows, hd = q.shape
    vd = 2 * hd
    v_blocks = vd // LANES
    width = k_new.shape[2]
    n_heads = width // vd
    n_pages = page_table.shape[1]
    tp = rows // (2 * n_heads)
    g = math.gcd(pages_per_step, n_pages)

    def page_spec(arr, slot, col):
        return pl.BlockSpec((1, arr.shape[1], LANES),
                            lambda b, s, pt: (pt[b * n_pages + s * g + slot], 0, col))

    row_spec = pl.BlockSpec((1, tp, width), lambda b, s, pt: (b, 0, 0))
    grid_spec = pltpu.PrefetchScalarGridSpec(
        num_scalar_prefetch=1,
        grid=(db, n_pages // g),
        in_specs=[
            pl.BlockSpec((4, hd), lambda b, s, pt: (0, 0)),
            pl.BlockSpec((1, rows, hd), lambda b, s, pt: (b, 0, 0)),
            row_spec, row_spec, row_spec,
            pl.BlockSpec((1, vd), lambda b, s, pt: (0, 0)),
            *[page_spec(cache_k, slot, 0) for slot in range(g)],
            *[page_spec(cache_v, slot, cb) for slot in range(g) for cb in range(v_blocks)],
        ],
        out_specs=row_spec,
        scratch_shapes=[pltpu.VMEM((rows, 1), _F32), pltpu.VMEM((rows, 1), _F32),
                        pltpu.VMEM((rows, vd), _F32)],
    )
    return pl.pallas_call(
        functools.partial(_attn_decode_kernel, pages_per_step=g, n_heads=n_heads, t_real=t_real, tp=tp,
                          scale=hd ** -0.5, lam_init=lam_init),
        grid_spec=grid_spec,
        out_shape=jax.ShapeDtypeStruct((db, tp, width), _BF16),
        compiler_params=_params("arbitrary", "arbitrary"),
        name="attn_decode",
    )(page_table.reshape(-1), lam, q, k_new, v_new, z, subln.reshape(1, vd),
      *([cache_k] * g), *([cache_v] * (g * v_blocks)))


def _out_proj_kernel(x_ref, g_ref, w_ref, y_ref):
    y_ref[...] = x_ref[...] + _dot(g_ref[...], w_ref[...])


def _out_proj(x, g, w, *, tm):
    m, d = x.shape
    a = g.shape[1]
    tm = min(tm, m)
    return pl.pallas_call(
        _out_proj_kernel,
        grid=(m // tm,),
        in_specs=[pl.BlockSpec((tm, d), lambda i: (i, 0)),
                  pl.BlockSpec((tm, a), lambda i: (i, 0)),
                  pl.BlockSpec((a, d), lambda i: (0, 0))],
        out_specs=pl.BlockSpec((tm, d), lambda i: (i, 0)),
        out_shape=jax.ShapeDtypeStruct((m, d), _F32),
        compiler_params=_params("arbitrary"),
        name="out_proj",
    )(x, g, w)


def _rope_tables(pos, hd, out_scale=1.0):
    rot = hd // 4
    half = rot // 2
    inv = ROPE_THETA ** (-jnp.arange(0, rot, 2, dtype=_F32) / rot)
    ang = pos.astype(_F32)[:, None] * inv[None, :]
    cos, sin = jnp.cos(ang), jnp.sin(ang)
    n = pos.shape[0]
    cos_t = jnp.concatenate([cos, cos, jnp.ones((n, hd - rot), _F32)], axis=1)
    sa_t = jnp.concatenate([-sin, jnp.zeros((n, hd - half), _F32)], axis=1)
    sb_t = jnp.concatenate([jnp.zeros((n, half), _F32), sin, jnp.zeros((n, hd - rot), _F32)], axis=1)
    return cos_t * out_scale, sa_t * out_scale, sb_t * out_scale


def _pad_rows(x, tp):
    return jnp.pad(x, ((0, 0), (0, tp - x.shape[1]), (0, 0)))


def kernel(x_prompt, x_sample, state_conv, cache_k, cache_v, page_table, norm_a, w_in_a, conv_w, w_out_a, norm_kv, w_kv, k_norm, norm_b, w_in_b, q_norm, lam, subln_w, w_out_b):
    bsz, seq, d = x_prompt.shape
    db, t_dec, _ = x_sample.shape
    n_a, n_b = norm_a.shape[0], norm_b.shape[0]
    n_phys, page, n_heads, _, hd = cache_k.shape
    n_pages = page_table.shape[1]
    past = n_pages * page
    width = n_heads * 2 * hd
    tp = -(-t_dec // SUBLANES) * SUBLANES
    tm = 512

    w_in_a16, w_out_a16 = w_in_a.astype(_BF16), w_out_a.astype(_BF16)
    w_kv16, w_in_b16, w_out_b16 = w_kv.astype(_BF16), w_in_b.astype(_BF16), w_out_b.astype(_BF16)
    pos_s = jnp.tile(past + jnp.arange(t_dec), db)
    tab_p = _rope_tables(jnp.arange(seq), hd)
    tab_s = _rope_tables(pos_s, hd)
    tab_pq = _rope_tables(jnp.arange(seq), hd, hd ** -0.5 * LOG2_E)
    cache_k3 = cache_k.reshape(n_phys, page * n_heads * 2, hd)
    cache_v3 = cache_v.reshape(n_phys, page * n_heads, 2 * hd)

    xp = x_prompt.reshape(bsz * seq, d)
    xs = x_sample.reshape(db * t_dec, d)
    tiles_per_seq = max(seq // tm, 1)
    conv_p, conv_s = [], []
    for layer in range(n_a):
        xp, up = _conv_layer(xp, norm_a[layer], w_in_a16[layer], conv_w[layer], w_out_a16[layer],
                             seq_rows=seq, tm=tm, tc=512)
        c_dim = up.shape[-1]
        conv_p.append(up.reshape(bsz, tiles_per_seq, SUBLANES, c_dim)[:, -1, SUBLANES - 2:, :])
        prev = state_conv[layer]
        zero = jnp.zeros_like(prev[:, :1])
        p1 = jnp.concatenate([prev[:, 1:2]] + [zero] * (t_dec - 1), axis=1)
        p2 = jnp.concatenate([prev[:, 0:1], prev[:, 1:2]] + [zero] * (t_dec - 2), axis=1)
        xs, us = _conv_layer(xs, norm_a[layer], w_in_a16[layer], conv_w[layer], w_out_a16[layer],
                             seq_rows=t_dec, prev=(p1.reshape(db * t_dec, c_dim), p2.reshape(db * t_dec, c_dim)),
                             tm=tm, tc=512)
        conv_s.append(us.reshape(db, t_dec, c_dim)[:, t_dec - 2:, :])

    proj = functools.partial(_proj, n=width, tm=tm)
    kp32, kp16 = proj(xp, norm_kv, w_kv16, col_block=0, out_kinds=("f32", "bf16"), name="k_prompt",
                      head_gain=k_norm, tables=tab_p, seq_rows=seq)
    vp32, vp16 = proj(xp, norm_kv, w_kv16, col_block=1, out_kinds=("f32", "bf16"), name="v_prompt")
    ks32, = proj(xs, norm_kv, w_kv16, col_block=0, out_kinds=("f32",), name="k_sample",
                 head_gain=k_norm, tables=tab_s, seq_rows=db * t_dec)
    vs32, = proj(xs, norm_kv, w_kv16, col_block=1, out_kinds=("f32",), name="v_sample")
    k_new = _pad_rows(ks32.reshape(db, t_dec, width), tp)
    v_new = _pad_rows(vs32.reshape(db, t_dec, width), tp)

    for j in range(n_b):
        lam_init = 0.8 - 0.6 * math.exp(-0.3 * (n_a + j))
        q16, = proj(xp, norm_b[j], w_in_b16[j], col_block=0, out_kinds=("bf16",), name="q_prompt",
                    head_gain=q_norm[j], tables=tab_pq, seq_rows=seq)
        z16, = proj(xp, norm_b[j], w_in_b16[j], col_block=1, out_kinds=("bf16",), name="z_prompt")
        gp = _attn_prompt(q16.reshape(bsz, seq, width), kp16.reshape(bsz, seq, width),
                          vp16.reshape(bsz, seq, width), z16.reshape(bsz, seq, width),
                          lam[j], subln_w[j], lam_init, hd=hd, tq=512)
        xp = _out_proj(xp, gp.reshape(bsz * seq, width), w_out_b16[j], tm=tm)

        qs32, = proj(xs, norm_b[j], w_in_b16[j], col_block=0, out_kinds=("f32",), name="q_sample",
                     head_gain=q_norm[j], tables=tab_s, seq_rows=db * t_dec)
        zs32, = proj(xs, norm_b[j], w_in_b16[j], col_block=1, out_kinds=("f32",), name="z_sample")
        qd = _pad_rows(qs32.reshape(db, t_dec, width), tp).reshape(db, tp, 2 * n_heads, hd)
        qd = qd.transpose(0, 2, 1, 3).reshape(db, 2 * n_heads * tp, hd)
        gs = _attn_decode(qd, cache_k3, cache_v3, page_table, k_new, v_new,
                          _pad_rows(zs32.reshape(db, t_dec, width), tp), lam[j], subln_w[j], lam_init,
                          t_real=t_dec, pages_per_step=4)
        xs = _out_proj(xs, gs[:, :t_dec].reshape(db * t_dec, width), w_out_b16[j], tm=tm)

    return (xp.reshape(bsz, seq, d), xs.reshape(db, t_dec, d),
            kp32.reshape(bsz, seq, n_heads, 2, hd), vp32.reshape(bsz, seq, n_heads, 2 * hd),
            jnp.stack(conv_p),
            ks32.reshape(db, t_dec, n_heads, 2, hd), vs32.reshape(db, t_dec, n_heads, 2 * hd),
            jnp.stack(conv_s))
```

```python
import functools
import math

import jax
import jax.numpy as jnp
from jax import lax
from jax.experimental import pallas as pl
from jax.experimental.pallas import tpu as pltpu

EPS = 1e-6
ROPE_THETA = 500000.0
LANES = 128
SUBLANES = 8
MXU_WIDTH = 256
VMEM_LIMIT_BYTES = 56 * 1024 * 1024
LOG2_E = math.log2(math.e)

_F32 = jnp.float32
_BF16 = jnp.bfloat16


def _dot(a, b):
    return jnp.dot(a, b, preferred_element_type=_F32)


def _dot_nt(a, b):
    return lax.dot_general(a, b, (((1,), (1,)), ((), ())), preferred_element_type=_F32)


def _silu(z):
    return z * (1.0 / (1.0 + jnp.exp(-z)))


def _rms_scale(x):
    return lax.rsqrt(jnp.mean(x * x, axis=-1, keepdims=True) + EPS)


def _params(*sem):
    return pltpu.CompilerParams(dimension_semantics=sem, vmem_limit_bytes=VMEM_LIMIT_BYTES)


def _conv_layer_kernel(*refs, prompt, tiles_per_seq, seq_rows):
    if prompt:
        (x_ref, g_ref, wb_ref, wc_ref, wh_ref, wz_ref, cw_ref, wo_ref,
         y_ref, u_ref, xn_scr, acc_scr, carry_scr) = refs
    else:
        (x_ref, g_ref, wb_ref, wc_ref, wh_ref, wz_ref, cw_ref, wo_ref, p1_ref, p2_ref,
         y_ref, u_ref, xn_scr, acc_scr) = refs
    i = pl.program_id(0)
    j = pl.program_id(1)
    nj = pl.num_programs(1)

    @pl.when(j == 0)
    def _():
        x = x_ref[...]
        xn_scr[...] = (x * _rms_scale(x) * g_ref[...]).astype(_BF16)

    xn = xn_scr[...]
    tm = xn.shape[0]
    gated = []
    for sub in range(wb_ref.shape[1] // MXU_WIDTH):
        cols = slice(sub * MXU_WIDTH, (sub + 1) * MXU_WIDTH)
        b = _dot(xn, wb_ref[:, cols])
        c = _dot(xn, wc_ref[:, cols])
        h = _dot(xn, wh_ref[:, cols])
        z = _dot(xn, wz_ref[:, cols])
        u = c * h
        row = lax.broadcasted_iota(jnp.int32, u.shape, 0)
        if prompt:
            first = (i % tiles_per_seq) == 0
            cr = carry_scr[j, :, cols]
            c0 = jnp.where(first, 0.0, cr[SUBLANES - 2:SUBLANES - 1, :])
            c1 = jnp.where(first, 0.0, cr[SUBLANES - 1:SUBLANES, :])
            um1 = jnp.where(row == 0, c1, pltpu.roll(u, 1, 0))
            um2 = jnp.where(row == 0, c0, jnp.where(row == 1, c1, pltpu.roll(u, 2, 0)))
            tail = u[tm - SUBLANES:tm, :]
            carry_scr[j, :, cols] = tail
            u_ref[0, :, cols] = tail
        else:
            t = row % seq_rows
            um1 = jnp.where(t == 0, p1_ref[:, cols], pltpu.roll(u, 1, 0))
            um2 = jnp.where(t <= 1, p2_ref[:, cols], pltpu.roll(u, 2, 0))
            u_ref[:, cols] = u
        conv = cw_ref[0:1, cols] * um2 + cw_ref[1:2, cols] * um1 + cw_ref[2:3, cols] * u
        gated.append((b * conv * _silu(z)).astype(_BF16))
    contrib = _dot(jnp.concatenate(gated, axis=1), wo_ref[...])

    @pl.when(j == 0)
    def _():
        acc_scr[...] = contrib

    @pl.when(j > 0)
    def _():
        acc_scr[...] += contrib

    @pl.when(j == nj - 1)
    def _():
        y_ref[...] = x_ref[...] + acc_scr[...]


def _conv_layer(x, norm_g, w_in, conv_w, w_out, *, layer, seq_rows, prev=None, tm, tc):
    m, d = x.shape
    c = conv_w.shape[2]
    prompt = prev is None
    tm = min(tm, m)
    tc = min(tc, c)
    ni, nj = m // tm, c // tc
    tiles_per_seq = max(seq_rows // tm, 1)

    def w_in_spec(gate):
        return pl.BlockSpec((None, d, tc), lambda i, j, gate=gate: (layer, 0, gate * nj + j))

    in_specs = [
        pl.BlockSpec((tm, d), lambda i, j: (i, 0)),
        pl.BlockSpec((1, d), lambda i, j: (0, 0)),
        w_in_spec(0), w_in_spec(1), w_in_spec(2), w_in_spec(3),
        pl.BlockSpec((None, 3, tc), lambda i, j: (layer, 0, j)),
        pl.BlockSpec((None, tc, d), lambda i, j: (layer, j, 0)),
    ]
    args = [x, norm_g.reshape(1, d), w_in, w_in, w_in, w_in, conv_w, w_out]
    scratch = [pltpu.VMEM((tm, d), _BF16), pltpu.VMEM((tm, d), _F32)]
    if prompt:
        u_shape = jax.ShapeDtypeStruct((ni, SUBLANES, c), _F32)
        u_spec = pl.BlockSpec((1, SUBLANES, tc), lambda i, j: (i, 0, j))
        scratch.append(pltpu.VMEM((nj, SUBLANES, tc), _F32))
    else:
        in_specs += [pl.BlockSpec((tm, tc), lambda i, j: (i, j))] * 2
        args += list(prev)
        u_shape = jax.ShapeDtypeStruct((m, c), _F32)
        u_spec = pl.BlockSpec((tm, tc), lambda i, j: (i, j))
    return pl.pallas_call(
        functools.partial(_conv_layer_kernel, prompt=prompt, tiles_per_seq=tiles_per_seq,
                          seq_rows=seq_rows),
        grid=(ni, nj),
        in_specs=in_specs,
        out_specs=[pl.BlockSpec((tm, d), lambda i, j: (i, 0)), u_spec],
        out_shape=[jax.ShapeDtypeStruct((m, d), _F32), u_shape],
        scratch_shapes=scratch,
        compiler_params=_params("arbitrary", "arbitrary"),
        name="conv_layer_prompt" if prompt else "conv_layer_sample",
    )(*args)


def _proj_heads_kernel(x_ref, g_ref, w_ref, hg_ref, cos_ref, sa_ref, sb_ref, *out_refs, out_kinds):
    outs = dict(zip(out_kinds, out_refs))
    x = x_ref[...]
    xn = (x * _rms_scale(x) * g_ref[...]).astype(_BF16)
    cos, sa, sb, hg = cos_ref[...], sa_ref[...], sb_ref[...], hg_ref[...]
    for chunk in range(w_ref.shape[1] // MXU_WIDTH):
        h = _dot(xn, w_ref[:, chunk * MXU_WIDTH:(chunk + 1) * MXU_WIDTH])
        for grp in range(MXU_WIDTH // LANES):
            sl = slice(chunk * MXU_WIDTH + grp * LANES, chunk * MXU_WIDTH + (grp + 1) * LANES)
            xg = h[:, grp * LANES:(grp + 1) * LANES]
            yn = xg * _rms_scale(xg) * hg
            y = yn * cos + pltpu.roll(yn, LANES - 16, 1) * sa + pltpu.roll(yn, 16, 1) * sb
            if "f32" in outs:
                outs["f32"][:, sl] = y
            if "f32_rows" in outs:
                n_groups = w_ref.shape[1] // LANES
                outs["f32_rows"][pl.ds(sl.start // LANES, x.shape[0], stride=n_groups), :] = y
            if "bf16" in outs:
                outs["bf16"][:, sl] = y.astype(_BF16)


def _proj_plain_kernel(x_ref, g_ref, w_ref, *out_refs, out_kinds):
    outs = dict(zip(out_kinds, out_refs))
    x = x_ref[...]
    xn = (x * _rms_scale(x) * g_ref[...]).astype(_BF16)
    h = _dot(xn, w_ref[...])
    if "f32" in outs:
        outs["f32"][...] = h
    if "bf16" in outs:
        outs["bf16"][...] = h.astype(_BF16)


def _proj(x, norm_g, w, *, layer, n, col_block, out_kinds, tm, name, head_gain=None, tables=None,
          seq_rows=None):
    m, d = x.shape
    tm = min(tm, m)
    row_spec = pl.BlockSpec((tm, n), lambda i: (i, 0))
    in_specs = [pl.BlockSpec((tm, d), lambda i: (i, 0)),
                pl.BlockSpec((1, d), lambda i: (0, 0)),
                pl.BlockSpec((None, d, n), lambda i: (layer, 0, col_block))]
    args = [x, norm_g.reshape(1, d), w]
    if head_gain is None:
        body = _proj_plain_kernel
    else:
        body = _proj_heads_kernel
        tiles_per_seq = max(seq_rows // tm, 1)
        tab_spec = pl.BlockSpec((tm, LANES), lambda i: (i % tiles_per_seq, 0))
        in_specs += [pl.BlockSpec((1, LANES), lambda i: (0, 0)), tab_spec, tab_spec, tab_spec]
        args += [head_gain.reshape(1, LANES), *tables]
    groups = n // LANES
    out_specs = [pl.BlockSpec((tm * groups, LANES), lambda i: (i, 0)) if kind == "f32_rows" else row_spec
                 for kind in out_kinds]
    out_shape = [jax.ShapeDtypeStruct((m * groups, LANES) if kind == "f32_rows" else (m, n),
                                      _BF16 if kind == "bf16" else _F32) for kind in out_kinds]
    return pl.pallas_call(
        functools.partial(body, out_kinds=tuple(out_kinds)),
        grid=(m // tm,),
        in_specs=in_specs,
        out_specs=out_specs,
        out_shape=out_shape,
        compiler_params=_params("arbitrary"),
        name=name,
    )(*args)


def _lambda_full(lam_ref, lam_init):
    lf = lam_ref[...]
    t1 = jnp.sum(lf[0:1, :] * lf[1:2, :], axis=-1, keepdims=True)
    t2 = jnp.sum(lf[2:3, :] * lf[3:4, :], axis=-1, keepdims=True)
    return jnp.exp(t1) - jnp.exp(t2) + lam_init


def _head_output(o1, o2, lam_full, z, subln, lam_init):
    o = o1 - lam_full * o2
    on = o * _rms_scale(o) * subln * (1.0 - lam_init)
    return on * _silu(z)


def _lane_repeat(x, n):
    return jnp.concatenate([x] * n, axis=1)


def _attn_prompt_kernel(lam_ref, q_ref, k_ref, v_ref, z_ref, sg_ref, o_ref,
                        kt_scr, s_scr, m_scr, l_scr, acc_scr, *, tq, hd, lam_init):
    qi = pl.program_id(2)
    n_kt = kt_scr.shape[0]
    vd = 2 * hd

    @pl.when(qi == 0)
    def _():
        for t in range(n_kt):
            kt_scr[t] = k_ref[0, t * tq:(t + 1) * tq, :].T

    q = q_ref[0]
    m_scr[...] = jnp.full(m_scr.shape, -jnp.inf, _F32)
    l_scr[...] = jnp.zeros(l_scr.shape, _F32)
    acc_scr[...] = jnp.zeros(acc_scr.shape, _F32)

    def scores(kt, slot):
        for c in range(2):
            s_scr[slot, c] = _dot(q[:, c * hd:(c + 1) * hd], kt_scr[kt, c * hd:(c + 1) * hd, :])

    def consume(kt, slot, masked):
        start = pl.multiple_of(kt * tq, tq)
        vs = v_ref[0, pl.ds(start, tq), :]
        for c in range(2):
            s = s_scr[slot, c]
            if masked:
                row = lax.broadcasted_iota(jnp.int32, s.shape, 0)
                col = lax.broadcasted_iota(jnp.int32, s.shape, 1)
                s = jnp.where(col <= row, s, -jnp.inf)
            m_prev = m_scr[c]
            m_new = jnp.maximum(m_prev, jnp.max(s, axis=-1, keepdims=True))
            alpha = jnp.exp2(m_prev - m_new)
            p = jnp.exp2(s - _lane_repeat(m_new, tq // LANES))
            l_scr[c] = alpha * l_scr[c] + jnp.sum(p, axis=-1, keepdims=True)
            acc_scr[c] = _lane_repeat(alpha, vd // LANES) * acc_scr[c] + _dot(p.astype(_BF16), vs)
            m_scr[c] = m_new

    scores(0, 0)

    def pair(i, carry):
        kt = 2 * i
        scores(kt + 1, 1)
        consume(kt, 0, False)
        scores(kt + 2, 0)
        consume(kt + 1, 1, False)
        return carry

    lax.fori_loop(0, qi // 2, pair, 0)

    @pl.when(qi % 2 == 0)
    def _():
        consume(qi, 0, True)

    @pl.when(qi % 2 == 1)
    def _():
        scores(qi, 1)
        consume(qi - 1, 0, False)
        consume(qi, 1, True)

    o1 = acc_scr[0] * _lane_repeat(1.0 / l_scr[0], vd // LANES)
    o2 = acc_scr[1] * _lane_repeat(1.0 / l_scr[1], vd // LANES)
    out = _head_output(o1, o2, _lambda_full(lam_ref, lam_init), z_ref[0].astype(_F32),
                       sg_ref[...], lam_init)
    o_ref[0] = out.astype(o_ref.dtype)


def _attn_prompt(q, k, v, z, lam, subln, lam_init, *, hd, tq):
    b, s, width = q.shape
    vd = 2 * hd
    n_heads = width // vd
    tq = min(tq, s)
    q_spec = pl.BlockSpec((1, tq, vd), lambda bi, h, qi: (bi, qi, h))
    kv_spec = pl.BlockSpec((1, s, vd), lambda bi, h, qi: (bi, 0, h))
    return pl.pallas_call(
        functools.partial(_attn_prompt_kernel, tq=tq, hd=hd, lam_init=lam_init),
        grid=(b, n_heads, s // tq),
        in_specs=[
            pl.BlockSpec((4, hd), lambda bi, h, qi: (0, 0)),
            q_spec, kv_spec, kv_spec, q_spec,
            pl.BlockSpec((1, vd), lambda bi, h, qi: (0, 0)),
        ],
        out_specs=q_spec,
        out_shape=jax.ShapeDtypeStruct((b, s, width), _BF16),
        scratch_shapes=[pltpu.VMEM((s // tq, vd, tq), _BF16), pltpu.VMEM((2, 2, tq, tq), _F32),
                        pltpu.VMEM((2, tq, LANES), _F32), pltpu.VMEM((2, tq, LANES), _F32),
                        pltpu.VMEM((2, tq, vd), _F32)],
        compiler_params=_params("arbitrary", "arbitrary", "arbitrary"),
        name="attn_prompt",
    )(lam, q, k, v, z, subln.reshape(1, vd))


def _decode_steps(cached_k, cached_v, lam_ref, q_ref, kn_ref, vn_ref, z_ref, sg_ref, o_ref,
                  m_scr, l_scr, acc_scr, *, n_heads, t_real, tp, scale, lam_init):
    hd = q_ref.shape[2]
    vd = 2 * hd
    step = pl.program_id(1)

    @pl.when(step == 0)
    def _():
        m_scr[...] = jnp.full(m_scr.shape, -jnp.inf, _F32)
        l_scr[...] = jnp.zeros(l_scr.shape, _F32)
        acc_scr[...] = jnp.zeros(acc_scr.shape, _F32)

    q = q_ref[0].astype(_BF16)

    def update(k_of_group, v_of_head, mask=None):
        s = jnp.concatenate([_dot_nt(q[g * tp:(g + 1) * tp, :], k_of_group(g)) for g in range(2 * n_heads)],
                            axis=0)
        if mask is not None:
            s = jnp.where(mask(s.shape), s, -jnp.inf)
        m_prev = m_scr[...]
        m_new = jnp.maximum(m_prev, jnp.max(s, axis=-1, keepdims=True))
        alpha = jnp.exp((m_prev - m_new) * scale)
        p = jnp.exp((s - m_new) * scale)
        l_scr[...] = alpha * l_scr[...] + jnp.sum(p, axis=-1, keepdims=True)
        p = p.astype(_BF16)
        pv = jnp.concatenate([_dot(p[h * 2 * tp:(h + 1) * 2 * tp, :], v_of_head(h)) for h in range(n_heads)],
                             axis=0)
        acc_scr[...] = alpha * acc_scr[...] + pv
        m_scr[...] = m_new

    update(cached_k, cached_v)

    @pl.when(step == pl.num_programs(1) - 1)
    def _():
        def new_key_mask(shape):
            t = lax.broadcasted_iota(jnp.int32, shape, 0) % tp
            jj = lax.broadcasted_iota(jnp.int32, shape, 1)
            return (jj <= t) & (jj < t_real)

        update(lambda g: kn_ref[0, :, g * hd:(g + 1) * hd].astype(_BF16),
               lambda h: vn_ref[0, :, h * vd:(h + 1) * vd].astype(_BF16), new_key_mask)
        lam_full = _lambda_full(lam_ref, lam_init)
        acc = acc_scr[...] * (1.0 / l_scr[...])
        for h in range(n_heads):
            r1 = (2 * h) * tp
            cols = slice(h * vd, (h + 1) * vd)
            out = _head_output(acc[r1:r1 + tp, :], acc[r1 + tp:r1 + 2 * tp, :], lam_full,
                               z_ref[0, :, cols], sg_ref[...], lam_init)
            o_ref[0, :, cols] = out.astype(o_ref.dtype)


def _attn_decode_paged_kernel(pt_ref, lam_ref, q_ref, kn_ref, vn_ref, z_ref, sg_ref, *refs,
                              pages_per_step, n_heads, **kw):
    del pt_ref
    hd = q_ref.shape[2]
    vd = 2 * hd
    v_blocks = vd // LANES
    k_rows = 2 * n_heads
    k_refs = refs[:pages_per_step]
    v_refs = refs[pages_per_step:(1 + v_blocks) * pages_per_step]
    o_ref, kd_ref, vd_ref, m_scr, l_scr, acc_scr = refs[(1 + v_blocks) * pages_per_step:]
    page = k_refs[0].shape[1] // k_rows

    def cached_k(g):
        k = jnp.concatenate([r[0, pl.ds(g, page, stride=k_rows), :] for r in k_refs], axis=0).astype(_BF16)
        kd_ref[0, :, g * hd:(g + 1) * hd] = k
        return k

    def cached_v(h):
        v = jnp.concatenate(
            [jnp.concatenate([v_refs[v_blocks * pg + cb][0, pl.ds(h, page, stride=n_heads), :]
                              for cb in range(v_blocks)], axis=1)
             for pg in range(pages_per_step)], axis=0).astype(_BF16)
        vd_ref[0, :, h * vd:(h + 1) * vd] = v
        return v

    _decode_steps(cached_k, cached_v, lam_ref, q_ref, kn_ref, vn_ref, z_ref, sg_ref, o_ref,
                  m_scr, l_scr, acc_scr, n_heads=n_heads, **kw)


def _attn_decode_dense_kernel(lam_ref, q_ref, kn_ref, vn_ref, z_ref, sg_ref, kd_ref, vd_ref, o_ref,
                              m_scr, l_scr, acc_scr, **kw):
    hd = q_ref.shape[2]
    vd = 2 * hd
    _decode_steps(lambda g: kd_ref[0, :, g * hd:(g + 1) * hd], lambda h: vd_ref[0, :, h * vd:(h + 1) * vd],
                  lam_ref, q_ref, kn_ref, vn_ref, z_ref, sg_ref, o_ref, m_scr, l_scr, acc_scr, **kw)


def _decode_scratch(rows, vd):
    return [pltpu.VMEM((rows, 1), _F32), pltpu.VMEM((rows, 1), _F32), pltpu.VMEM((rows, vd), _F32)]


def _attn_decode_paged(q, cache_k, cache_v, page_table, k_new, v_new, z, lam, subln, lam_init, *,
                       t_real, pages_per_step):
    db, rows, hd = q.shape
    vd = 2 * hd
    v_blocks = vd // LANES
    width = k_new.shape[2]
    n_heads = width // vd
    n_pages = page_table.shape[1]
    page = cache_v.shape[1] // n_heads
    tp = rows // (2 * n_heads)
    g = math.gcd(pages_per_step, n_pages)

    def page_spec(arr, slot, col):
        return pl.BlockSpec((1, arr.shape[1], LANES),
                            lambda b, s, pt: (pt[b * n_pages + s * g + slot], 0, col))

    row_spec = pl.BlockSpec((1, tp, width), lambda b, s, pt: (b, 0, 0))
    dense_spec = pl.BlockSpec((1, g * page, width), lambda b, s, pt: (b, s, 0))
    dense_shape = jax.ShapeDtypeStruct((db, n_pages * page, width), _BF16)
    grid_spec = pltpu.PrefetchScalarGridSpec(
        num_scalar_prefetch=1,
        grid=(db, n_pages // g),
        in_specs=[
            pl.BlockSpec((4, hd), lambda b, s, pt: (0, 0)),
            pl.BlockSpec((1, rows, hd), lambda b, s, pt: (b, 0, 0)),
            row_spec, row_spec, row_spec,
            pl.BlockSpec((1, vd), lambda b, s, pt: (0, 0)),
            *[page_spec(cache_k, slot, 0) for slot in range(g)],
            *[page_spec(cache_v, slot, cb) for slot in range(g) for cb in range(v_blocks)],
        ],
        out_specs=[row_spec, dense_spec, dense_spec],
        scratch_shapes=_decode_scratch(rows, vd),
    )
    return pl.pallas_call(
        functools.partial(_attn_decode_paged_kernel, pages_per_step=g, n_heads=n_heads, t_real=t_real,
                          tp=tp, scale=hd ** -0.5, lam_init=lam_init),
        grid_spec=grid_spec,
        out_shape=[jax.ShapeDtypeStruct((db, tp, width), _BF16), dense_shape, dense_shape],
        compiler_params=_params("arbitrary", "arbitrary"),
        name="attn_decode_paged",
    )(page_table.reshape(-1), lam, q, k_new, v_new, z, subln.reshape(1, vd),
      *([cache_k] * g), *([cache_v] * (g * v_blocks)))


def _attn_decode_dense(q, k_past, v_past, k_new, v_new, z, lam, subln, lam_init, *, t_real, tk):
    db, rows, hd = q.shape
    vd = 2 * hd
    past, width = k_past.shape[1:]
    n_heads = width // vd
    tp = rows // (2 * n_heads)
    tk = math.gcd(tk, past)
    row_spec = pl.BlockSpec((1, tp, width), lambda b, s: (b, 0, 0))
    dense_spec = pl.BlockSpec((1, tk, width), lambda b, s: (b, s, 0))
    return pl.pallas_call(
        functools.partial(_attn_decode_dense_kernel, n_heads=n_heads, t_real=t_real, tp=tp,
                          scale=hd ** -0.5, lam_init=lam_init),
        grid=(db, past // tk),
        in_specs=[
            pl.BlockSpec((4, hd), lambda b, s: (0, 0)),
            pl.BlockSpec((1, rows, hd), lambda b, s: (b, 0, 0)),
            row_spec, row_spec, row_spec,
            pl.BlockSpec((1, vd), lambda b, s: (0, 0)),
            dense_spec, dense_spec,
        ],
        out_specs=row_spec,
        out_shape=jax.ShapeDtypeStruct((db, tp, width), _BF16),
        scratch_shapes=_decode_scratch(rows, vd),
        compiler_params=_params("arbitrary", "arbitrary"),
        name="attn_decode_dense",
    )(lam, q, k_new, v_new, z, subln.reshape(1, vd), k_past, v_past)


def _out_proj_kernel(x_ref, g_ref, w_ref, y_ref):
    y_ref[...] = x_ref[...] + _dot(g_ref[...], w_ref[...])


def _out_proj(x, g, w, *, layer, tm):
    m, d = x.shape
    a = g.shape[1]
    tm = min(tm, m)
    return pl.pallas_call(
        _out_proj_kernel,
        grid=(m // tm,),
        in_specs=[pl.BlockSpec((tm, d), lambda i: (i, 0)),
                  pl.BlockSpec((tm, a), lambda i: (i, 0)),
                  pl.BlockSpec((None, a, d), lambda i: (layer, 0, 0))],
        out_specs=pl.BlockSpec((tm, d), lambda i: (i, 0)),
        out_shape=jax.ShapeDtypeStruct((m, d), _F32),
        compiler_params=_params("arbitrary"),
        name="out_proj",
    )(x, g, w)


def _rope_tables(pos, hd, out_scale=1.0):
    rot = hd // 4
    half = rot // 2
    inv = ROPE_THETA ** (-jnp.arange(0, rot, 2, dtype=_F32) / rot)
    ang = pos.astype(_F32)[:, None] * inv[None, :]
    cos, sin = jnp.cos(ang), jnp.sin(ang)
    n = pos.shape[0]
    cos_t = jnp.concatenate([cos, cos, jnp.ones((n, hd - rot), _F32)], axis=1)
    sa_t = jnp.concatenate([-sin, jnp.zeros((n, hd - half), _F32)], axis=1)
    sb_t = jnp.concatenate([jnp.zeros((n, half), _F32), sin, jnp.zeros((n, hd - rot), _F32)], axis=1)
    return cos_t * out_scale, sa_t * out_scale, sb_t * out_scale


def _pad_rows(x, tp):
    return jnp.pad(x, ((0, 0), (0, tp - x.shape[1]), (0, 0)))


def kernel(x_prompt, x_sample, state_conv, cache_k, cache_v, page_table, norm_a, w_in_a, conv_w, w_out_a, norm_kv, w_kv, k_norm, norm_b, w_in_b, q_norm, lam, subln_w, w_out_b):
    bsz, seq, d = x_prompt.shape
    db, t_dec, _ = x_sample.shape
    n_a, n_b = norm_a.shape[0], norm_b.shape[0]
    n_phys, page, n_heads, _, hd = cache_k.shape
    n_pages = page_table.shape[1]
    past = n_pages * page
    width = n_heads * 2 * hd
    tp = -(-t_dec // SUBLANES) * SUBLANES
    tm = 512

    w_in_a16, w_out_a16 = w_in_a.astype(_BF16), w_out_a.astype(_BF16)
    w_kv16, w_in_b16, w_out_b16 = w_kv.astype(_BF16), w_in_b.astype(_BF16), w_out_b.astype(_BF16)
    pos_s = jnp.tile(past + jnp.arange(t_dec), db)
    tab_p = _rope_tables(jnp.arange(seq), hd)
    tab_s = _rope_tables(pos_s, hd)
    tab_pq = _rope_tables(jnp.arange(seq), hd, hd ** -0.5 * LOG2_E)
    cache_k3 = cache_k.reshape(n_phys, page * n_heads * 2, hd)
    cache_v3 = cache_v.reshape(n_phys, page * n_heads, 2 * hd)

    xp = x_prompt.reshape(bsz * seq, d)
    xs = x_sample.reshape(db * t_dec, d)
    tiles_per_seq = max(seq // tm, 1)
    conv_p, conv_s = [], []
    for layer in range(n_a):
        xp, up = _conv_layer(xp, norm_a[layer], w_in_a16, conv_w, w_out_a16, layer=layer,
                             seq_rows=seq, tm=tm, tc=512)
        c_dim = up.shape[-1]
        conv_p.append(up.reshape(bsz, tiles_per_seq, SUBLANES, c_dim)[:, -1, SUBLANES - 2:, :])
        prev = state_conv[layer]
        zero = jnp.zeros_like(prev[:, :1])
        p1 = jnp.concatenate([prev[:, 1:2]] + [zero] * (t_dec - 1), axis=1)
        p2 = jnp.concatenate([prev[:, 0:1], prev[:, 1:2]] + [zero] * (t_dec - 2), axis=1)
        xs, us = _conv_layer(xs, norm_a[layer], w_in_a16, conv_w, w_out_a16, layer=layer,
                             seq_rows=t_dec, prev=(p1.reshape(db * t_dec, c_dim), p2.reshape(db * t_dec, c_dim)),
                             tm=tm, tc=512)
        conv_s.append(us.reshape(db, t_dec, c_dim)[:, t_dec - 2:, :])

    proj = functools.partial(_proj, n=width, tm=tm)
    w_kv16 = w_kv16[None]
    kp32, kp16 = proj(xp, norm_kv, w_kv16, layer=0, col_block=0, out_kinds=("f32_rows", "bf16"), name="k_prompt",
                      head_gain=k_norm, tables=tab_p, seq_rows=seq)
    vp32, vp16 = proj(xp, norm_kv, w_kv16, layer=0, col_block=1, out_kinds=("f32", "bf16"), name="v_prompt")
    ks32, = proj(xs, norm_kv, w_kv16, layer=0, col_block=0, out_kinds=("f32",), name="k_sample",
                 head_gain=k_norm, tables=tab_s, seq_rows=db * t_dec)
    vs32, = proj(xs, norm_kv, w_kv16, layer=0, col_block=1, out_kinds=("f32",), name="v_sample")
    k_new = _pad_rows(ks32.reshape(db, t_dec, width), tp)
    v_new = _pad_rows(vs32.reshape(db, t_dec, width), tp)

    for j in range(n_b):
        lam_init = 0.8 - 0.6 * math.exp(-0.3 * (n_a + j))
        q16, = proj(xp, norm_b[j], w_in_b16, layer=j, col_block=0, out_kinds=("bf16",), name="q_prompt",
                    head_gain=q_norm[j], tables=tab_pq, seq_rows=seq)
        z16, = proj(xp, norm_b[j], w_in_b16, layer=j, col_block=1, out_kinds=("bf16",), name="z_prompt")
        gp = _attn_prompt(q16.reshape(bsz, seq, width), kp16.reshape(bsz, seq, width),
                          vp16.reshape(bsz, seq, width), z16.reshape(bsz, seq, width),
                          lam[j], subln_w[j], lam_init, hd=hd, tq=512)
        xp = _out_proj(xp, gp.reshape(bsz * seq, width), w_out_b16, layer=j, tm=tm)

        qs32, = proj(xs, norm_b[j], w_in_b16, layer=j, col_block=0, out_kinds=("f32",), name="q_sample",
                     head_gain=q_norm[j], tables=tab_s, seq_rows=db * t_dec)
        zs32, = proj(xs, norm_b[j], w_in_b16, layer=j, col_block=1, out_kinds=("f32",), name="z_sample")
        qd = _pad_rows(qs32.reshape(db, t_dec, width), tp).reshape(db, tp, 2 * n_heads, hd)
        qd = qd.transpose(0, 2, 1, 3).reshape(db, 2 * n_heads * tp, hd)
        zd = _pad_rows(zs32.reshape(db, t_dec, width), tp)
        if j == 0:
            gs, k_past16, v_past16 = _attn_decode_paged(qd, cache_k3, cache_v3, page_table, k_new, v_new, zd,
                                                        lam[j], subln_w[j], lam_init, t_real=t_dec,
                                                        pages_per_step=4)
        else:
            gs = _attn_decode_dense(qd, k_past16, v_past16, k_new, v_new, zd, lam[j], subln_w[j], lam_init,
                                    t_real=t_dec, tk=1024)
        xs = _out_proj(xs, gs[:, :t_dec].reshape(db * t_dec, width), w_out_b16, layer=j, tm=tm)

    return (xp.reshape(bsz, seq, d), xs.reshape(db, t_dec, d),
            kp32.reshape(bsz, seq, n_heads, 2, hd), vp32.reshape(bsz, seq, n_heads, 2 * hd),
            jnp.stack(conv_p),
            ks32.reshape(db, t_dec, n_heads, 2, hd), vs32.reshape(db, t_dec, n_heads, 2 * hd),
            jnp.stack(conv_s))
```

```python
import functools
import math

import jax
import jax.numpy as jnp
from jax import lax
from jax.experimental import pallas as pl
from jax.experimental.pallas import tpu as pltpu

EPS = 1e-6
ROPE_THETA = 500000.0
LANES = 128
SUBLANES = 8
MXU_WIDTH = 256
VMEM_LIMIT_BYTES = 56 * 1024 * 1024
LOG2_E = math.log2(math.e)

_F32 = jnp.float32
_BF16 = jnp.bfloat16


def _dot(a, b):
    return jnp.dot(a, b, preferred_element_type=_F32)


def _dot_nt(a, b):
    return lax.dot_general(a, b, (((1,), (1,)), ((), ())), preferred_element_type=_F32)


def _silu(z):
    return z * (1.0 / (1.0 + jnp.exp(-z)))


def _rms_scale(x):
    return lax.rsqrt(jnp.mean(x * x, axis=-1, keepdims=True) + EPS)


def _params(*sem):
    return pltpu.CompilerParams(dimension_semantics=sem, vmem_limit_bytes=VMEM_LIMIT_BYTES)


def _conv_layer_kernel(*refs, prompt, tiles_per_seq, seq_rows):
    if prompt:
        (x_ref, g_ref, wb_ref, wc_ref, wh_ref, wz_ref, cw_ref, wo_ref,
         y_ref, u_ref, xn_scr, carry_scr) = refs
    else:
        (x_ref, g_ref, wb_ref, wc_ref, wh_ref, wz_ref, cw_ref, wo_ref, p1_ref, p2_ref,
         y_ref, u_ref, xn_scr) = refs
    i = pl.program_id(0)
    j = pl.program_id(1)

    @pl.when(j == 0)
    def _():
        x = x_ref[...]
        xn_scr[...] = (x * _rms_scale(x) * g_ref[...]).astype(_BF16)
        y_ref[...] = x

    xn = xn_scr[...]
    tm = xn.shape[0]
    gated = []
    for sub in range(wb_ref.shape[1] // MXU_WIDTH):
        cols = slice(sub * MXU_WIDTH, (sub + 1) * MXU_WIDTH)
        b = _dot(xn, wb_ref[:, cols])
        c = _dot(xn, wc_ref[:, cols])
        h = _dot(xn, wh_ref[:, cols])
        z = _dot(xn, wz_ref[:, cols])
        u = c * h
        row = lax.broadcasted_iota(jnp.int32, u.shape, 0)
        if prompt:
            first = (i % tiles_per_seq) == 0
            cr = carry_scr[j, :, cols]
            c0 = jnp.where(first, 0.0, cr[SUBLANES - 2:SUBLANES - 1, :])
            c1 = jnp.where(first, 0.0, cr[SUBLANES - 1:SUBLANES, :])
            um1 = jnp.where(row == 0, c1, pltpu.roll(u, 1, 0))
            um2 = jnp.where(row == 0, c0, jnp.where(row == 1, c1, pltpu.roll(u, 2, 0)))
            tail = u[tm - SUBLANES:tm, :]
            carry_scr[j, :, cols] = tail
            u_ref[0, :, cols] = tail
        else:
            t = row % seq_rows
            um1 = jnp.where(t == 0, p1_ref[:, cols], pltpu.roll(u, 1, 0))
            um2 = jnp.where(t <= 1, p2_ref[:, cols], pltpu.roll(u, 2, 0))
            u_ref[:, cols] = u
        conv = cw_ref[0:1, cols] * um2 + cw_ref[1:2, cols] * um1 + cw_ref[2:3, cols] * u
        gated.append((b * conv * _silu(z)).astype(_BF16))
    y_ref[...] += _dot(jnp.concatenate(gated, axis=1), wo_ref[...])


def _conv_layer(x, norm_g, w_in, conv_w, w_out, *, layer, seq_rows, prev=None, tm, tc):
    m, d = x.shape
    c = conv_w.shape[2]
    prompt = prev is None
    tm = min(tm, m)
    tc = min(tc, c)
    ni, nj = m // tm, c // tc
    tiles_per_seq = max(seq_rows // tm, 1)

    def w_in_spec(gate):
        return pl.BlockSpec((None, d, tc), lambda i, j, gate=gate: (layer, 0, gate * nj + j))

    in_specs = [
        pl.BlockSpec((tm, d), lambda i, j: (i, 0)),
        pl.BlockSpec((1, d), lambda i, j: (0, 0)),
        w_in_spec(0), w_in_spec(1), w_in_spec(2), w_in_spec(3),
        pl.BlockSpec((None, 3, tc), lambda i, j: (layer, 0, j)),
        pl.BlockSpec((None, tc, d), lambda i, j: (layer, j, 0)),
    ]
    args = [x, norm_g.reshape(1, d), w_in, w_in, w_in, w_in, conv_w, w_out]
    scratch = [pltpu.VMEM((tm, d), _BF16)]
    if prompt:
        u_shape = jax.ShapeDtypeStruct((ni, SUBLANES, c), _F32)
        u_spec = pl.BlockSpec((1, SUBLANES, tc), lambda i, j: (i, 0, j))
        scratch.append(pltpu.VMEM((nj, SUBLANES, tc), _F32))
    else:
        in_specs += [pl.BlockSpec((tm, tc), lambda i, j: (i, j))] * 2
        args += list(prev)
        u_shape = jax.ShapeDtypeStruct((m, c), _F32)
        u_spec = pl.BlockSpec((tm, tc), lambda i, j: (i, j))
    return pl.pallas_call(
        functools.partial(_conv_layer_kernel, prompt=prompt, tiles_per_seq=tiles_per_seq,
                          seq_rows=seq_rows),
        grid=(ni, nj),
        in_specs=in_specs,
        out_specs=[pl.BlockSpec((tm, d), lambda i, j: (i, 0)), u_spec],
        out_shape=[jax.ShapeDtypeStruct((m, d), _F32), u_shape],
        scratch_shapes=scratch,
        compiler_params=_params("arbitrary", "arbitrary"),
        name="conv_layer_prompt" if prompt else "conv_layer_sample",
    )(*args)


def _proj_heads_kernel(x_ref, g_ref, w_ref, hg_ref, cos_ref, sa_ref, sb_ref, *out_refs, out_kinds):
    outs = dict(zip(out_kinds, out_refs))
    x = x_ref[...]
    xn = (x * _rms_scale(x) * g_ref[...]).astype(_BF16)
    cos, sa, sb, hg = cos_ref[...], sa_ref[...], sb_ref[...], hg_ref[...]
    for chunk in range(w_ref.shape[1] // MXU_WIDTH):
        h = _dot(xn, w_ref[:, chunk * MXU_WIDTH:(chunk + 1) * MXU_WIDTH])
        for grp in range(MXU_WIDTH // LANES):
            sl = slice(chunk * MXU_WIDTH + grp * LANES, chunk * MXU_WIDTH + (grp + 1) * LANES)
            xg = h[:, grp * LANES:(grp + 1) * LANES]
            yn = xg * _rms_scale(xg) * hg
            y = yn * cos + pltpu.roll(yn, LANES - 16, 1) * sa + pltpu.roll(yn, 16, 1) * sb
            if "f32" in outs:
                outs["f32"][:, sl] = y
            if "f32_rows" in outs:
                n_groups = w_ref.shape[1] // LANES
                outs["f32_rows"][pl.ds(sl.start // LANES, x.shape[0], stride=n_groups), :] = y
            if "bf16" in outs:
                outs["bf16"][:, sl] = y.astype(_BF16)


def _proj_plain_kernel(x_ref, g_ref, w_ref, *out_refs, out_kinds):
    outs = dict(zip(out_kinds, out_refs))
    x = x_ref[...]
    xn = (x * _rms_scale(x) * g_ref[...]).astype(_BF16)
    h = _dot(xn, w_ref[...])
    if "f32" in outs:
        outs["f32"][...] = h
    if "bf16" in outs:
        outs["bf16"][...] = h.astype(_BF16)


def _proj(x, norm_g, w, *, layer, n, col_block, out_kinds, tm, name, head_gain=None, tables=None,
          seq_rows=None):
    m, d = x.shape
    tm = min(tm, m)
    row_spec = pl.BlockSpec((tm, n), lambda i: (i, 0))
    in_specs = [pl.BlockSpec((tm, d), lambda i: (i, 0)),
                pl.BlockSpec((1, d), lambda i: (0, 0)),
                pl.BlockSpec((None, d, n), lambda i: (layer, 0, col_block))]
    args = [x, norm_g.reshape(1, d), w]
    if head_gain is None:
        body = _proj_plain_kernel
    else:
        body = _proj_heads_kernel
        tiles_per_seq = max(seq_rows // tm, 1)
        tab_spec = pl.BlockSpec((tm, LANES), lambda i: (i % tiles_per_seq, 0))
        in_specs += [pl.BlockSpec((1, LANES), lambda i: (0, 0)), tab_spec, tab_spec, tab_spec]
        args += [head_gain.reshape(1, LANES), *tables]
    groups = n // LANES
    out_specs = [pl.BlockSpec((tm * groups, LANES), lambda i: (i, 0)) if kind == "f32_rows" else row_spec
                 for kind in out_kinds]
    out_shape = [jax.ShapeDtypeStruct((m * groups, LANES) if kind == "f32_rows" else (m, n),
                                      _BF16 if kind == "bf16" else _F32) for kind in out_kinds]
    return pl.pallas_call(
        functools.partial(body, out_kinds=tuple(out_kinds)),
        grid=(m // tm,),
        in_specs=in_specs,
        out_specs=out_specs,
        out_shape=out_shape,
        compiler_params=_params("arbitrary"),
        name=name,
    )(*args)


def _lambda_full(lam_ref, lam_init):
    lf = lam_ref[...]
    t1 = jnp.sum(lf[0:1, :] * lf[1:2, :], axis=-1, keepdims=True)
    t2 = jnp.sum(lf[2:3, :] * lf[3:4, :], axis=-1, keepdims=True)
    return jnp.exp(t1) - jnp.exp(t2) + lam_init


def _head_output(o1, o2, lam_full, z, subln, lam_init):
    o = o1 - lam_full * o2
    on = o * _rms_scale(o) * subln * (1.0 - lam_init)
    return on * _silu(z)


def _lane_repeat(x, n):
    return jnp.concatenate([x] * n, axis=1)


def _attn_prompt_kernel(lam_ref, q_ref, k_ref, v_ref, z_ref, sg_ref, o_ref,
                        kt_scr, s_scr, m_scr, l_scr, acc_scr, *, tq, hd, lam_init):
    qi = pl.program_id(2)
    n_kt = kt_scr.shape[0]
    vd = 2 * hd

    @pl.when(qi == 0)
    def _():
        for t in range(n_kt):
            kt_scr[t] = k_ref[0, t * tq:(t + 1) * tq, :].T

    q = q_ref[0]
    m_scr[...] = jnp.full(m_scr.shape, -jnp.inf, _F32)
    l_scr[...] = jnp.zeros(l_scr.shape, _F32)
    acc_scr[...] = jnp.zeros(acc_scr.shape, _F32)

    def scores(kt, slot):
        for c in range(2):
            s_scr[slot, c] = _dot(q[:, c * hd:(c + 1) * hd], kt_scr[kt, c * hd:(c + 1) * hd, :])

    def consume(kt, slot, masked):
        start = pl.multiple_of(kt * tq, tq)
        vs = v_ref[0, pl.ds(start, tq), :]
        for c in range(2):
            s = s_scr[slot, c]
            if masked:
                row = lax.broadcasted_iota(jnp.int32, s.shape, 0)
                col = lax.broadcasted_iota(jnp.int32, s.shape, 1)
                s = jnp.where(col <= row, s, -jnp.inf)
            m_prev = m_scr[c]
            m_new = jnp.maximum(m_prev, jnp.max(s, axis=-1, keepdims=True))
            alpha = jnp.exp2(m_prev - m_new)
            p = jnp.exp2(s - _lane_repeat(m_new, tq // LANES))
            l_scr[c] = alpha * l_scr[c] + jnp.sum(p, axis=-1, keepdims=True)
            acc_scr[c] = _lane_repeat(alpha, vd // LANES) * acc_scr[c] + _dot(p.astype(_BF16), vs)
            m_scr[c] = m_new

    scores(0, 0)

    def pair(i, carry):
        kt = 2 * i
        scores(kt + 1, 1)
        consume(kt, 0, False)
        scores(kt + 2, 0)
        consume(kt + 1, 1, False)
        return carry

    lax.fori_loop(0, qi // 2, pair, 0)

    @pl.when(qi % 2 == 0)
    def _():
        consume(qi, 0, True)

    @pl.when(qi % 2 == 1)
    def _():
        scores(qi, 1)
        consume(qi - 1, 0, False)
        consume(qi, 1, True)

    o1 = acc_scr[0] * _lane_repeat(1.0 / l_scr[0], vd // LANES)
    o2 = acc_scr[1] * _lane_repeat(1.0 / l_scr[1], vd // LANES)
    out = _head_output(o1, o2, _lambda_full(lam_ref, lam_init), z_ref[0].astype(_F32),
                       sg_ref[...], lam_init)
    o_ref[0] = out.astype(o_ref.dtype)


def _attn_prompt(q, k, v, z, lam, subln, lam_init, *, hd, tq):
    b, s, width = q.shape
    vd = 2 * hd
    n_heads = width // vd
    tq = min(tq, s)
    q_spec = pl.BlockSpec((1, tq, vd), lambda bi, h, qi: (bi, qi, h))
    kv_spec = pl.BlockSpec((1, s, vd), lambda bi, h, qi: (bi, 0, h))
    return pl.pallas_call(
        functools.partial(_attn_prompt_kernel, tq=tq, hd=hd, lam_init=lam_init),
        grid=(b, n_heads, s // tq),
        in_specs=[
            pl.BlockSpec((4, hd), lambda bi, h, qi: (0, 0)),
            q_spec, kv_spec, kv_spec, q_spec,
            pl.BlockSpec((1, vd), lambda bi, h, qi: (0, 0)),
        ],
        out_specs=q_spec,
        out_shape=jax.ShapeDtypeStruct((b, s, width), _BF16),
        scratch_shapes=[pltpu.VMEM((s // tq, vd, tq), _BF16), pltpu.VMEM((2, 2, tq, tq), _F32),
                        pltpu.VMEM((2, tq, LANES), _F32), pltpu.VMEM((2, tq, LANES), _F32),
                        pltpu.VMEM((2, tq, vd), _F32)],
        compiler_params=_params("arbitrary", "arbitrary", "arbitrary"),
        name="attn_prompt",
    )(lam, q, k, v, z, subln.reshape(1, vd))


def _decode_steps(cached_k, cached_v, lam_ref, q_ref, kn_ref, vn_ref, z_ref, sg_ref, o_ref,
                  m_scr, l_scr, acc_scr, *, n_heads, t_real, tp, scale, lam_init):
    hd = q_ref.shape[2]
    vd = 2 * hd
    step = pl.program_id(1)

    @pl.when(step == 0)
    def _():
        m_scr[...] = jnp.full(m_scr.shape, -jnp.inf, _F32)
        l_scr[...] = jnp.zeros(l_scr.shape, _F32)
        acc_scr[...] = jnp.zeros(acc_scr.shape, _F32)

    q = q_ref[0].astype(_BF16)

    def update(k_of_group, v_of_head, mask=None):
        s = jnp.concatenate([_dot_nt(q[g * tp:(g + 1) * tp, :], k_of_group(g)) for g in range(2 * n_heads)],
                            axis=0)
        if mask is not None:
            s = jnp.where(mask(s.shape), s, -jnp.inf)
        m_prev = m_scr[...]
        m_new = jnp.maximum(m_prev, jnp.max(s, axis=-1, keepdims=True))
        alpha = jnp.exp((m_prev - m_new) * scale)
        p = jnp.exp((s - m_new) * scale)
        l_scr[...] = alpha * l_scr[...] + jnp.sum(p, axis=-1, keepdims=True)
        p = p.astype(_BF16)
        pv = jnp.concatenate([_dot(p[h * 2 * tp:(h + 1) * 2 * tp, :], v_of_head(h)) for h in range(n_heads)],
                             axis=0)
        acc_scr[...] = alpha * acc_scr[...] + pv
        m_scr[...] = m_new

    update(cached_k, cached_v)

    @pl.when(step == pl.num_programs(1) - 1)
    def _():
        def new_key_mask(shape):
            t = lax.broadcasted_iota(jnp.int32, shape, 0) % tp
            jj = lax.broadcasted_iota(jnp.int32, shape, 1)
            return (jj <= t) & (jj < t_real)

        update(lambda g: kn_ref[0, :, g * hd:(g + 1) * hd].astype(_BF16),
               lambda h: vn_ref[0, :, h * vd:(h + 1) * vd].astype(_BF16), new_key_mask)
        lam_full = _lambda_full(lam_ref, lam_init)
        acc = acc_scr[...] * (1.0 / l_scr[...])
        for h in range(n_heads):
            r1 = (2 * h) * tp
            cols = slice(h * vd, (h + 1) * vd)
            out = _head_output(acc[r1:r1 + tp, :], acc[r1 + tp:r1 + 2 * tp, :], lam_full,
                               z_ref[0, :, cols], sg_ref[...], lam_init)
            o_ref[0, :, cols] = out.astype(o_ref.dtype)


def _attn_decode_paged_kernel(pt_ref, lam_ref, q_ref, kn_ref, vn_ref, z_ref, sg_ref, *refs,
                              pages_per_step, n_heads, **kw):
    del pt_ref
    hd = q_ref.shape[2]
    vd = 2 * hd
    v_blocks = vd // LANES
    k_rows = 2 * n_heads
    k_refs = refs[:pages_per_step]
    v_refs = refs[pages_per_step:(1 + v_blocks) * pages_per_step]
    o_ref, kd_ref, vd_ref, m_scr, l_scr, acc_scr = refs[(1 + v_blocks) * pages_per_step:]
    page = k_refs[0].shape[1] // k_rows

    def cached_k(g):
        k = jnp.concatenate([r[0, pl.ds(g, page, stride=k_rows), :] for r in k_refs], axis=0).astype(_BF16)
        kd_ref[0, :, g * hd:(g + 1) * hd] = k
        return k

    def cached_v(h):
        v = jnp.concatenate(
            [jnp.concatenate([v_refs[v_blocks * pg + cb][0, pl.ds(h, page, stride=n_heads), :]
                              for cb in range(v_blocks)], axis=1)
             for pg in range(pages_per_step)], axis=0).astype(_BF16)
        vd_ref[0, :, h * vd:(h + 1) * vd] = v
        return v

    _decode_steps(cached_k, cached_v, lam_ref, q_ref, kn_ref, vn_ref, z_ref, sg_ref, o_ref,
                  m_scr, l_scr, acc_scr, n_heads=n_heads, **kw)


def _attn_decode_dense_kernel(lam_ref, q_ref, kn_ref, vn_ref, z_ref, sg_ref, kd_ref, vd_ref, o_ref,
                              m_scr, l_scr, acc_scr, **kw):
    hd = q_ref.shape[2]
    vd = 2 * hd
    _decode_steps(lambda g: kd_ref[0, :, g * hd:(g + 1) * hd], lambda h: vd_ref[0, :, h * vd:(h + 1) * vd],
                  lam_ref, q_ref, kn_ref, vn_ref, z_ref, sg_ref, o_ref, m_scr, l_scr, acc_scr, **kw)


def _decode_scratch(rows, vd):
    return [pltpu.VMEM((rows, 1), _F32), pltpu.VMEM((rows, 1), _F32), pltpu.VMEM((rows, vd), _F32)]


def _attn_decode_paged(q, cache_k, cache_v, page_table, k_new, v_new, z, lam, subln, lam_init, *,
                       t_real, pages_per_step):
    db, rows, hd = q.shape
    vd = 2 * hd
    v_blocks = vd // LANES
    width = k_new.shape[2]
    n_heads = width // vd
    n_pages = page_table.shape[1]
    page = cache_v.shape[1] // n_heads
    tp = rows // (2 * n_heads)
    g = math.gcd(pages_per_step, n_pages)

    def page_spec(arr, slot, col):
        return pl.BlockSpec((1, arr.shape[1], LANES),
                            lambda b, s, pt: (pt[b * n_pages + s * g + slot], 0, col))

    row_spec = pl.BlockSpec((1, tp, width), lambda b, s, pt: (b, 0, 0))
    dense_spec = pl.BlockSpec((1, g * page, width), lambda b, s, pt: (b, s, 0))
    dense_shape = jax.ShapeDtypeStruct((db, n_pages * page, width), _BF16)
    grid_spec = pltpu.PrefetchScalarGridSpec(
        num_scalar_prefetch=1,
        grid=(db, n_pages // g),
        in_specs=[
            pl.BlockSpec((4, hd), lambda b, s, pt: (0, 0)),
            pl.BlockSpec((1, rows, hd), lambda b, s, pt: (b, 0, 0)),
            row_spec, row_spec, row_spec,
            pl.BlockSpec((1, vd), lambda b, s, pt: (0, 0)),
            *[page_spec(cache_k, slot, 0) for slot in range(g)],
            *[page_spec(cache_v, slot, cb) for slot in range(g) for cb in range(v_blocks)],
        ],
        out_specs=[row_spec, dense_spec, dense_spec],
        scratch_shapes=_decode_scratch(rows, vd),
    )
    return pl.pallas_call(
        functools.partial(_attn_decode_paged_kernel, pages_per_step=g, n_heads=n_heads, t_real=t_real,
                          tp=tp, scale=hd ** -0.5, lam_init=lam_init),
        grid_spec=grid_spec,
        out_shape=[jax.ShapeDtypeStruct((db, tp, width), _BF16), dense_shape, dense_shape],
        compiler_params=_params("arbitrary", "arbitrary"),
        name="attn_decode_paged",
    )(page_table.reshape(-1), lam, q, k_new, v_new, z, subln.reshape(1, vd),
      *([cache_k] * g), *([cache_v] * (g * v_blocks)))


def _attn_decode_dense(q, k_past, v_past, k_new, v_new, z, lam, subln, lam_init, *, t_real, tk):
    db, rows, hd = q.shape
    vd = 2 * hd
    past, width = k_past.shape[1:]
    n_heads = width // vd
    tp = rows // (2 * n_heads)
    tk = math.gcd(tk, past)
    row_spec = pl.BlockSpec((1, tp, width), lambda b, s: (b, 0, 0))
    dense_spec = pl.BlockSpec((1, tk, width), lambda b, s: (b, s, 0))
    return pl.pallas_call(
        functools.partial(_attn_decode_dense_kernel, n_heads=n_heads, t_real=t_real, tp=tp,
                          scale=hd ** -0.5, lam_init=lam_init),
        grid=(db, past // tk),
        in_specs=[
            pl.BlockSpec((4, hd), lambda b, s: (0, 0)),
            pl.BlockSpec((1, rows, hd), lambda b, s: (b, 0, 0)),
            row_spec, row_spec, row_spec,
            pl.BlockSpec((1, vd), lambda b, s: (0, 0)),
            dense_spec, dense_spec,
        ],
        out_specs=row_spec,
        out_shape=jax.ShapeDtypeStruct((db, tp, width), _BF16),
        scratch_shapes=_decode_scratch(rows, vd),
        compiler_params=_params("arbitrary", "arbitrary"),
        name="attn_decode_dense",
    )(lam, q, k_new, v_new, z, subln.reshape(1, vd), k_past, v_past)


def _out_proj_kernel(x_ref, g_ref, w_ref, y_ref):
    y_ref[...] = x_ref[...] + _dot(g_ref[...], w_ref[...])


def _out_proj(x, g, w, *, layer, tm):
    m, d = x.shape
    a = g.shape[1]
    tm = min(tm, m)
    return pl.pallas_call(
        _out_proj_kernel,
        grid=(m // tm,),
        in_specs=[pl.BlockSpec((tm, d), lambda i: (i, 0)),
                  pl.BlockSpec((tm, a), lambda i: (i, 0)),
                  pl.BlockSpec((None, a, d), lambda i: (layer, 0, 0))],
        out_specs=pl.BlockSpec((tm, d), lambda i: (i, 0)),
        out_shape=jax.ShapeDtypeStruct((m, d), _F32),
        compiler_params=_params("arbitrary"),
        name="out_proj",
    )(x, g, w)


def _rope_tables(pos, hd, out_scale=1.0):
    rot = hd // 4
    half = rot // 2
    inv = ROPE_THETA ** (-jnp.arange(0, rot, 2, dtype=_F32) / rot)
    ang = pos.astype(_F32)[:, None] * inv[None, :]
    cos, sin = jnp.cos(ang), jnp.sin(ang)
    n = pos.shape[0]
    cos_t = jnp.concatenate([cos, cos, jnp.ones((n, hd - rot), _F32)], axis=1)
    sa_t = jnp.concatenate([-sin, jnp.zeros((n, hd - half), _F32)], axis=1)
    sb_t = jnp.concatenate([jnp.zeros((n, half), _F32), sin, jnp.zeros((n, hd - rot), _F32)], axis=1)
    return cos_t * out_scale, sa_t * out_scale, sb_t * out_scale


def _pad_rows(x, tp):
    return jnp.pad(x, ((0, 0), (0, tp - x.shape[1]), (0, 0)))


def kernel(x_prompt, x_sample, state_conv, cache_k, cache_v, page_table, norm_a, w_in_a, conv_w, w_out_a, norm_kv, w_kv, k_norm, norm_b, w_in_b, q_norm, lam, subln_w, w_out_b):
    bsz, seq, d = x_prompt.shape
    db, t_dec, _ = x_sample.shape
    n_a, n_b = norm_a.shape[0], norm_b.shape[0]
    n_phys, page, n_heads, _, hd = cache_k.shape
    n_pages = page_table.shape[1]
    past = n_pages * page
    width = n_heads * 2 * hd
    tp = -(-t_dec // SUBLANES) * SUBLANES
    tm = 512

    w_in_a16, w_out_a16 = w_in_a.astype(_BF16), w_out_a.astype(_BF16)
    w_kv16, w_in_b16, w_out_b16 = w_kv.astype(_BF16), w_in_b.astype(_BF16), w_out_b.astype(_BF16)
    pos_s = jnp.tile(past + jnp.arange(t_dec), db)
    tab_p = _rope_tables(jnp.arange(seq), hd)
    tab_s = _rope_tables(pos_s, hd)
    tab_pq = _rope_tables(jnp.arange(seq), hd, hd ** -0.5 * LOG2_E)
    cache_k3 = cache_k.reshape(n_phys, page * n_heads * 2, hd)
    cache_v3 = cache_v.reshape(n_phys, page * n_heads, 2 * hd)

    xp = x_prompt.reshape(bsz * seq, d)
    xs = x_sample.reshape(db * t_dec, d)
    tiles_per_seq = max(seq // tm, 1)
    conv_p, conv_s = [], []
    for layer in range(n_a):
        xp, up = _conv_layer(xp, norm_a[layer], w_in_a16, conv_w, w_out_a16, layer=layer,
                             seq_rows=seq, tm=tm, tc=512)
        c_dim = up.shape[-1]
        conv_p.append(up.reshape(bsz, tiles_per_seq, SUBLANES, c_dim)[:, -1, SUBLANES - 2:, :])
        prev = state_conv[layer]
        zero = jnp.zeros_like(prev[:, :1])
        p1 = jnp.concatenate([prev[:, 1:2]] + [zero] * (t_dec - 1), axis=1)
        p2 = jnp.concatenate([prev[:, 0:1], prev[:, 1:2]] + [zero] * (t_dec - 2), axis=1)
        xs, us = _conv_layer(xs, norm_a[layer], w_in_a16, conv_w, w_out_a16, layer=layer,
                             seq_rows=t_dec, prev=(p1.reshape(db * t_dec, c_dim), p2.reshape(db * t_dec, c_dim)),
                             tm=tm, tc=512)
        conv_s.append(us.reshape(db, t_dec, c_dim)[:, t_dec - 2:, :])

    proj = functools.partial(_proj, n=width, tm=tm)
    w_kv16 = w_kv16[None]
    kp32, kp16 = proj(xp, norm_kv, w_kv16, layer=0, col_block=0, out_kinds=("f32_rows", "bf16"), name="k_prompt",
                      head_gain=k_norm, tables=tab_p, seq_rows=seq)
    vp32, vp16 = proj(xp, norm_kv, w_kv16, layer=0, col_block=1, out_kinds=("f32", "bf16"), name="v_prompt")
    ks32, = proj(xs, norm_kv, w_kv16, layer=0, col_block=0, out_kinds=("f32",), name="k_sample",
                 head_gain=k_norm, tables=tab_s, seq_rows=db * t_dec)
    vs32, = proj(xs, norm_kv, w_kv16, layer=0, col_block=1, out_kinds=("f32",), name="v_sample")
    k_new = _pad_rows(ks32.reshape(db, t_dec, width), tp)
    v_new = _pad_rows(vs32.reshape(db, t_dec, width), tp)

    for j in range(n_b):
        lam_init = 0.8 - 0.6 * math.exp(-0.3 * (n_a + j))
        q16, = proj(xp, norm_b[j], w_in_b16, layer=j, col_block=0, out_kinds=("bf16",), name="q_prompt",
                    head_gain=q_norm[j], tables=tab_pq, seq_rows=seq)
        z16, = proj(xp, norm_b[j], w_in_b16, layer=j, col_block=1, out_kinds=("bf16",), name="z_prompt")
        gp = _attn_prompt(q16.reshape(bsz, seq, width), kp16.reshape(bsz, seq, width),
                          vp16.reshape(bsz, seq, width), z16.reshape(bsz, seq, width),
                          lam[j], subln_w[j], lam_init, hd=hd, tq=512)
        xp = _out_proj(xp, gp.reshape(bsz * seq, width), w_out_b16, layer=j, tm=tm)

        qs32, = proj(xs, norm_b[j], w_in_b16, layer=j, col_block=0, out_kinds=("f32",), name="q_sample",
                     head_gain=q_norm[j], tables=tab_s, seq_rows=db * t_dec)
        zs32, = proj(xs, norm_b[j], w_in_b16, layer=j, col_block=1, out_kinds=("f32",), name="z_sample")
        qd = _pad_rows(qs32.reshape(db, t_dec, width), tp).reshape(db, tp, 2 * n_heads, hd)
        qd = qd.transpose(0, 2, 1, 3).reshape(db, 2 * n_heads * tp, hd)
        zd = _pad_rows(zs32.reshape(db, t_dec, width), tp)
        if j == 0:
            gs, k_past16, v_past16 = _attn_decode_paged(qd, cache_k3, cache_v3, page_table, k_new, v_new, zd,
                                                        lam[j], subln_w[j], lam_init, t_real=t_dec,
                                                        pages_per_step=8)
        else:
            gs = _attn_decode_dense(qd, k_past16, v_past16, k_new, v_new, zd, lam[j], subln_w[j], lam_init,
                                    t_real=t_dec, tk=2048)
        xs = _out_proj(xs, gs[:, :t_dec].reshape(db * t_dec, width), w_out_b16, layer=j, tm=tm)

    return (xp.reshape(bsz, seq, d), xs.reshape(db, t_dec, d),
            kp32.reshape(bsz, seq, n_heads, 2, hd), vp32.reshape(bsz, seq, n_heads, 2 * hd),
            jnp.stack(conv_p),
            ks32.reshape(db, t_dec, n_heads, 2, hd), vs32.reshape(db, t_dec, n_heads, 2 * hd),
            jnp.stack(conv_s))
```

```python
import functools
import math

import jax
import jax.numpy as jnp
from jax import lax
from jax.experimental import pallas as pl
from jax.experimental.pallas import tpu as pltpu

EPS = 1e-6
ROPE_THETA = 500000.0
LANES = 128
SUBLANES = 8
MXU_WIDTH = 256
VMEM_LIMIT_BYTES = 56 * 1024 * 1024
LOG2_E = math.log2(math.e)

_F32 = jnp.float32
_BF16 = jnp.bfloat16


def _dot(a, b):
    return jnp.dot(a, b, preferred_element_type=_F32)


def _dot_nt(a, b):
    return lax.dot_general(a, b, (((1,), (1,)), ((), ())), preferred_element_type=_F32)


def _silu(z):
    return z * (1.0 / (1.0 + jnp.exp(-z)))


def _rms_scale(x):
    return lax.rsqrt(jnp.mean(x * x, axis=-1, keepdims=True) + EPS)


def _params(*sem):
    return pltpu.CompilerParams(dimension_semantics=sem, vmem_limit_bytes=VMEM_LIMIT_BYTES)


def _conv_layer_kernel(*refs, prompt, tiles_per_seq, seq_rows):
    if prompt:
        (x_ref, g_ref, wb_ref, wc_ref, wh_ref, wz_ref, cw_ref, wo_ref,
         y_ref, u_ref, xn_scr, carry_scr) = refs
    else:
        (x_ref, g_ref, wb_ref, wc_ref, wh_ref, wz_ref, cw_ref, wo_ref, p1_ref, p2_ref,
         y_ref, u_ref, xn_scr) = refs
    i = pl.program_id(0)
    j = pl.program_id(1)

    @pl.when(j == 0)
    def _():
        x = x_ref[...]
        xn_scr[...] = (x * _rms_scale(x) * g_ref[...]).astype(_BF16)
        y_ref[...] = x

    xn = xn_scr[...]
    tm = xn.shape[0]
    gated = []
    for sub in range(wb_ref.shape[1] // MXU_WIDTH):
        cols = slice(sub * MXU_WIDTH, (sub + 1) * MXU_WIDTH)
        b = _dot(xn, wb_ref[:, cols])
        c = _dot(xn, wc_ref[:, cols])
        h = _dot(xn, wh_ref[:, cols])
        z = _dot(xn, wz_ref[:, cols])
        u = c * h
        row = lax.broadcasted_iota(jnp.int32, u.shape, 0)
        if prompt:
            first = (i % tiles_per_seq) == 0
            cr = carry_scr[j, :, cols]
            c0 = jnp.where(first, 0.0, cr[SUBLANES - 2:SUBLANES - 1, :])
            c1 = jnp.where(first, 0.0, cr[SUBLANES - 1:SUBLANES, :])
            um1 = jnp.where(row == 0, c1, pltpu.roll(u, 1, 0))
            um2 = jnp.where(row == 0, c0, jnp.where(row == 1, c1, pltpu.roll(u, 2, 0)))
            tail = u[tm - SUBLANES:tm, :]
            carry_scr[j, :, cols] = tail
            u_ref[0, :, cols] = tail
        else:
            t = row % seq_rows
            um1 = jnp.where(t == 0, p1_ref[:, cols], pltpu.roll(u, 1, 0))
            um2 = jnp.where(t <= 1, p2_ref[:, cols], pltpu.roll(u, 2, 0))
            u_ref[:, cols] = u
        conv = cw_ref[0:1, cols] * um2 + cw_ref[1:2, cols] * um1 + cw_ref[2:3, cols] * u
        gated.append((b * conv * _silu(z)).astype(_BF16))
    y_ref[...] += _dot(jnp.concatenate(gated, axis=1), wo_ref[...])


def _conv_layer(x, norm_g, w_in, conv_w, w_out, *, layer, seq_rows, prev=None, tm, tc):
    m, d = x.shape
    c = conv_w.shape[2]
    prompt = prev is None
    tm = min(tm, m)
    tc = min(tc, c)
    ni, nj = m // tm, c // tc
    tiles_per_seq = max(seq_rows // tm, 1)

    def w_in_spec(gate):
        return pl.BlockSpec((None, d, tc), lambda i, j, gate=gate: (layer, 0, gate * nj + j))

    in_specs = [
        pl.BlockSpec((tm, d), lambda i, j: (i, 0)),
        pl.BlockSpec((1, d), lambda i, j: (0, 0)),
        w_in_spec(0), w_in_spec(1), w_in_spec(2), w_in_spec(3),
        pl.BlockSpec((None, 3, tc), lambda i, j: (layer, 0, j)),
        pl.BlockSpec((None, tc, d), lambda i, j: (layer, j, 0)),
    ]
    args = [x, norm_g.reshape(1, d), w_in, w_in, w_in, w_in, conv_w, w_out]
    scratch = [pltpu.VMEM((tm, d), _BF16)]
    if prompt:
        u_shape = jax.ShapeDtypeStruct((ni, SUBLANES, c), _F32)
        u_spec = pl.BlockSpec((1, SUBLANES, tc), lambda i, j: (i, 0, j))
        scratch.append(pltpu.VMEM((nj, SUBLANES, tc), _F32))
    else:
        in_specs += [pl.BlockSpec((tm, tc), lambda i, j: (i, j))] * 2
        args += list(prev)
        u_shape = jax.ShapeDtypeStruct((m, c), _F32)
        u_spec = pl.BlockSpec((tm, tc), lambda i, j: (i, j))
    return pl.pallas_call(
        functools.partial(_conv_layer_kernel, prompt=prompt, tiles_per_seq=tiles_per_seq,
                          seq_rows=seq_rows),
        grid=(ni, nj),
        in_specs=in_specs,
        out_specs=[pl.BlockSpec((tm, d), lambda i, j: (i, 0)), u_spec],
        out_shape=[jax.ShapeDtypeStruct((m, d), _F32), u_shape],
        scratch_shapes=scratch,
        compiler_params=_params("arbitrary", "arbitrary"),
        name="conv_layer_prompt" if prompt else "conv_layer_sample",
    )(*args)


def _proj_heads_kernel(x_ref, g_ref, w_ref, hg_ref, cos_ref, sa_ref, sb_ref, *out_refs, out_kinds):
    outs = dict(zip(out_kinds, out_refs))
    x = x_ref[...]
    xn = (x * _rms_scale(x) * g_ref[...]).astype(_BF16)
    cos, sa, sb, hg = cos_ref[...], sa_ref[...], sb_ref[...], hg_ref[...]
    for chunk in range(w_ref.shape[1] // MXU_WIDTH):
        h = _dot(xn, w_ref[:, chunk * MXU_WIDTH:(chunk + 1) * MXU_WIDTH])
        for grp in range(MXU_WIDTH // LANES):
            sl = slice(chunk * MXU_WIDTH + grp * LANES, chunk * MXU_WIDTH + (grp + 1) * LANES)
            xg = h[:, grp * LANES:(grp + 1) * LANES]
            yn = xg * _rms_scale(xg) * hg
            y = yn * cos + pltpu.roll(yn, LANES - 16, 1) * sa + pltpu.roll(yn, 16, 1) * sb
            if "f32" in outs:
                outs["f32"][:, sl] = y
            if "f32_rows" in outs:
                n_groups = w_ref.shape[1] // LANES
                outs["f32_rows"][pl.ds(sl.start // LANES, x.shape[0], stride=n_groups), :] = y
            if "bf16" in outs:
                outs["bf16"][:, sl] = y.astype(_BF16)


def _proj_plain_kernel(x_ref, g_ref, w_ref, *out_refs, out_kinds):
    outs = dict(zip(out_kinds, out_refs))
    x = x_ref[...]
    xn = (x * _rms_scale(x) * g_ref[...]).astype(_BF16)
    h = _dot(xn, w_ref[...])
    if "f32" in outs:
        outs["f32"][...] = h
    if "bf16" in outs:
        outs["bf16"][...] = h.astype(_BF16)


def _proj(x, norm_g, w, *, layer, n, col_block, out_kinds, tm, name, head_gain=None, tables=None,
          seq_rows=None):
    m, d = x.shape
    tm = min(tm, m)
    row_spec = pl.BlockSpec((tm, n), lambda i: (i, 0))
    in_specs = [pl.BlockSpec((tm, d), lambda i: (i, 0)),
                pl.BlockSpec((1, d), lambda i: (0, 0)),
                pl.BlockSpec((None, d, n), lambda i: (layer, 0, col_block))]
    args = [x, norm_g.reshape(1, d), w]
    if head_gain is None:
        body = _proj_plain_kernel
    else:
        body = _proj_heads_kernel
        tiles_per_seq = max(seq_rows // tm, 1)
        tab_spec = pl.BlockSpec((tm, LANES), lambda i: (i % tiles_per_seq, 0))
        in_specs += [pl.BlockSpec((1, LANES), lambda i: (0, 0)), tab_spec, tab_spec, tab_spec]
        args += [head_gain.reshape(1, LANES), *tables]
    groups = n // LANES
    out_specs = [pl.BlockSpec((tm * groups, LANES), lambda i: (i, 0)) if kind == "f32_rows" else row_spec
                 for kind in out_kinds]
    out_shape = [jax.ShapeDtypeStruct((m * groups, LANES) if kind == "f32_rows" else (m, n),
                                      _BF16 if kind == "bf16" else _F32) for kind in out_kinds]
    return pl.pallas_call(
        functools.partial(body, out_kinds=tuple(out_kinds)),
        grid=(m // tm,),
        in_specs=in_specs,
        out_specs=out_specs,
        out_shape=out_shape,
        compiler_params=_params("arbitrary"),
        name=name,
    )(*args)


def _lambda_full(lam_ref, lam_init):
    lf = lam_ref[...]
    t1 = jnp.sum(lf[0:1, :] * lf[1:2, :], axis=-1, keepdims=True)
    t2 = jnp.sum(lf[2:3, :] * lf[3:4, :], axis=-1, keepdims=True)
    return jnp.exp(t1) - jnp.exp(t2) + lam_init


def _head_output(o1, o2, lam_full, z, subln, lam_init):
    o = o1 - lam_full * o2
    on = o * _rms_scale(o) * subln * (1.0 - lam_init)
    return on * _silu(z)


def _lane_repeat(x, n):
    return jnp.concatenate([x] * n, axis=1)


def _attn_prompt_kernel(lam_ref, q_ref, k_ref, v_ref, z_ref, sg_ref, o_ref,
                        kt_scr, s_scr, m_scr, l_scr, acc_scr, *, tq, hd, lam_init):
    qi = pl.program_id(2)
    n_kt = kt_scr.shape[0]
    vd = 2 * hd

    @pl.when(qi == 0)
    def _():
        for t in range(n_kt):
            kt_scr[t] = k_ref[0, t * tq:(t + 1) * tq, :].T

    q = q_ref[0]
    m_scr[...] = jnp.full(m_scr.shape, -jnp.inf, _F32)
    l_scr[...] = jnp.zeros(l_scr.shape, _F32)
    acc_scr[...] = jnp.zeros(acc_scr.shape, _F32)

    def scores(kt, slot):
        for c in range(2):
            s_scr[slot, c] = _dot(q[:, c * hd:(c + 1) * hd], kt_scr[kt, c * hd:(c + 1) * hd, :])

    def consume(kt, slot, masked):
        start = pl.multiple_of(kt * tq, tq)
        vs = v_ref[0, pl.ds(start, tq), :]
        for c in range(2):
            s = s_scr[slot, c]
            if masked:
                row = lax.broadcasted_iota(jnp.int32, s.shape, 0)
                col = lax.broadcasted_iota(jnp.int32, s.shape, 1)
                s = jnp.where(col <= row, s, -jnp.inf)
            m_prev = m_scr[c]
            m_new = jnp.maximum(m_prev, jnp.max(s, axis=-1, keepdims=True))
            alpha = jnp.exp2(m_prev - m_new)
            p = jnp.exp2(s - _lane_repeat(m_new, tq // LANES))
            l_scr[c] = alpha * l_scr[c] + jnp.sum(p, axis=-1, keepdims=True)
            acc_scr[c] = _lane_repeat(alpha, vd // LANES) * acc_scr[c] + _dot(p.astype(_BF16), vs)
            m_scr[c] = m_new

    scores(0, 0)

    def pair(i, carry):
        kt = 2 * i
        scores(kt + 1, 1)
        consume(kt, 0, False)
        scores(kt + 2, 0)
        consume(kt + 1, 1, False)
        return carry

    lax.fori_loop(0, qi // 2, pair, 0)

    @pl.when(qi % 2 == 0)
    def _():
        consume(qi, 0, True)

    @pl.when(qi % 2 == 1)
    def _():
        scores(qi, 1)
        consume(qi - 1, 0, False)
        consume(qi, 1, True)

    o1 = acc_scr[0] * _lane_repeat(1.0 / l_scr[0], vd // LANES)
    o2 = acc_scr[1] * _lane_repeat(1.0 / l_scr[1], vd // LANES)
    out = _head_output(o1, o2, _lambda_full(lam_ref, lam_init), z_ref[0].astype(_F32),
                       sg_ref[...], lam_init)
    o_ref[0] = out.astype(o_ref.dtype)


def _attn_prompt(q, k, v, z, lam, subln, lam_init, *, hd, tq):
    b, s, width = q.shape
    vd = 2 * hd
    n_heads = width // vd
    tq = min(tq, s)
    q_spec = pl.BlockSpec((1, tq, vd), lambda bi, h, qi: (bi, qi, h))
    kv_spec = pl.BlockSpec((1, s, vd), lambda bi, h, qi: (bi, 0, h))
    return pl.pallas_call(
        functools.partial(_attn_prompt_kernel, tq=tq, hd=hd, lam_init=lam_init),
        grid=(b, n_heads, s // tq),
        in_specs=[
            pl.BlockSpec((4, hd), lambda bi, h, qi: (0, 0)),
            q_spec, kv_spec, kv_spec, q_spec,
            pl.BlockSpec((1, vd), lambda bi, h, qi: (0, 0)),
        ],
        out_specs=q_spec,
        out_shape=jax.ShapeDtypeStruct((b, s, width), _BF16),
        scratch_shapes=[pltpu.VMEM((s // tq, vd, tq), _BF16), pltpu.VMEM((2, 2, tq, tq), _F32),
                        pltpu.VMEM((2, tq, LANES), _F32), pltpu.VMEM((2, tq, LANES), _F32),
                        pltpu.VMEM((2, tq, vd), _F32)],
        compiler_params=_params("arbitrary", "arbitrary", "arbitrary"),
        name="attn_prompt",
    )(lam, q, k, v, z, subln.reshape(1, vd))


def _decode_steps(cached_k, cached_v, lam_ref, q_ref, kn_ref, vn_ref, z_ref, sg_ref, o_ref,
                  m_scr, l_scr, acc_scr, *, n_heads, t_real, tp, scale, lam_init):
    hd = q_ref.shape[2]
    vd = 2 * hd
    step = pl.program_id(1)

    @pl.when(step == 0)
    def _():
        m_scr[...] = jnp.full(m_scr.shape, -jnp.inf, _F32)
        l_scr[...] = jnp.zeros(l_scr.shape, _F32)
        acc_scr[...] = jnp.zeros(acc_scr.shape, _F32)

    q = q_ref[0].astype(_BF16)

    def update(k_of_group, v_of_head, mask=None):
        s = jnp.concatenate([_dot_nt(q[g * tp:(g + 1) * tp, :], k_of_group(g)) for g in range(2 * n_heads)],
                            axis=0)
        if mask is not None:
            s = jnp.where(mask(s.shape), s, -jnp.inf)
        m_prev = m_scr[...]
        m_new = jnp.maximum(m_prev, jnp.max(s, axis=-1, keepdims=True))
        alpha = jnp.exp((m_prev - m_new) * scale)
        p = jnp.exp((s - m_new) * scale)
        l_scr[...] = alpha * l_scr[...] + jnp.sum(p, axis=-1, keepdims=True)
        p = p.astype(_BF16)
        pv = jnp.concatenate([_dot(p[h * 2 * tp:(h + 1) * 2 * tp, :], v_of_head(h)) for h in range(n_heads)],
                             axis=0)
        acc_scr[...] = alpha * acc_scr[...] + pv
        m_scr[...] = m_new

    update(cached_k, cached_v)

    @pl.when(step == pl.num_programs(1) - 1)
    def _():
        def new_key_mask(shape):
            t = lax.broadcasted_iota(jnp.int32, shape, 0) % tp
            jj = lax.broadcasted_iota(jnp.int32, shape, 1)
            return (jj <= t) & (jj < t_real)

        update(lambda g: kn_ref[0, :, g * hd:(g + 1) * hd].astype(_BF16),
               lambda h: vn_ref[0, :, h * vd:(h + 1) * vd].astype(_BF16), new_key_mask)
        lam_full = _lambda_full(lam_ref, lam_init)
        acc = acc_scr[...] * (1.0 / l_scr[...])
        for h in range(n_heads):
            r1 = (2 * h) * tp
            cols = slice(h * vd, (h + 1) * vd)
            out = _head_output(acc[r1:r1 + tp, :], acc[r1 + tp:r1 + 2 * tp, :], lam_full,
                               z_ref[0, :, cols], sg_ref[...], lam_init)
            o_ref[0, :, cols] = out.astype(o_ref.dtype)


PAGE_RING_SLOTS = 3


def _page_copies(pt_ref, k_hbm, v_hbm, kbuf, vbuf, sem, step, slot, *, pages_per_step):
    v_blocks = v_hbm.shape[2] // LANES
    copies = []
    for g in range(pages_per_step):
        pid = pt_ref[step * pages_per_step + g]
        i = slot * pages_per_step + g
        copies.append(pltpu.make_async_copy(k_hbm.at[pid], kbuf.at[i], sem.at[slot]))
        for cb in range(v_blocks):
            copies.append(pltpu.make_async_copy(v_hbm.at[pid, :, pl.ds(cb * LANES, LANES)],
                                                vbuf.at[i * v_blocks + cb], sem.at[slot]))
    return copies


def _attn_decode_paged_kernel(pt_ref, lam_ref, q_ref, kn_ref, vn_ref, z_ref, sg_ref, k_hbm, v_hbm,
                              o_ref, kd_ref, vd_ref, kbuf, vbuf, sem, m_scr, l_scr, acc_scr, *,
                              pages_per_step, n_heads, **kw):
    hd = q_ref.shape[2]
    vd = 2 * hd
    v_blocks = vd // LANES
    k_rows = 2 * n_heads
    page = k_hbm.shape[1] // k_rows
    n_steps = pl.num_programs(1)
    total = pl.num_programs(0) * n_steps
    lin = pl.program_id(0) * n_steps + pl.program_id(1)
    slot = lin % PAGE_RING_SLOTS
    copies = functools.partial(_page_copies, pt_ref, k_hbm, v_hbm, kbuf, vbuf, sem,
                               pages_per_step=pages_per_step)

    @pl.when(lin == 0)
    def _():
        for ahead in range(PAGE_RING_SLOTS - 1):
            @pl.when(ahead < total)
            def _():
                for cp in copies(ahead, ahead):
                    cp.start()

    @pl.when(lin + PAGE_RING_SLOTS - 1 < total)
    def _():
        nxt = lin + PAGE_RING_SLOTS - 1
        for cp in copies(nxt, nxt % PAGE_RING_SLOTS):
            cp.start()

    for cp in copies(lin, slot):
        cp.wait()

    def cached_k(g):
        k = jnp.concatenate([kbuf[slot * pages_per_step + pg, pl.ds(g, page, stride=k_rows), :]
                             for pg in range(pages_per_step)], axis=0).astype(_BF16)
        kd_ref[0, :, g * hd:(g + 1) * hd] = k
        return k

    def cached_v(h):
        v = jnp.concatenate(
            [jnp.concatenate([vbuf[(slot * pages_per_step + pg) * v_blocks + cb,
                                   pl.ds(h, page, stride=n_heads), :] for cb in range(v_blocks)], axis=1)
             for pg in range(pages_per_step)], axis=0).astype(_BF16)
        vd_ref[0, :, h * vd:(h + 1) * vd] = v
        return v

    _decode_steps(cached_k, cached_v, lam_ref, q_ref, kn_ref, vn_ref, z_ref, sg_ref, o_ref,
                  m_scr, l_scr, acc_scr, n_heads=n_heads, **kw)


def _attn_decode_dense_kernel(lam_ref, q_ref, kn_ref, vn_ref, z_ref, sg_ref, kd_ref, vd_ref, o_ref,
                              m_scr, l_scr, acc_scr, **kw):
    hd = q_ref.shape[2]
    vd = 2 * hd
    _decode_steps(lambda g: kd_ref[0, :, g * hd:(g + 1) * hd], lambda h: vd_ref[0, :, h * vd:(h + 1) * vd],
                  lam_ref, q_ref, kn_ref, vn_ref, z_ref, sg_ref, o_ref, m_scr, l_scr, acc_scr, **kw)


def _decode_scratch(rows, vd):
    return [pltpu.VMEM((rows, 1), _F32), pltpu.VMEM((rows, 1), _F32), pltpu.VMEM((rows, vd), _F32)]


def _attn_decode_paged(q, cache_k, cache_v, page_table, k_new, v_new, z, lam, subln, lam_init, *,
                       t_real, pages_per_step):
    db, rows, hd = q.shape
    vd = 2 * hd
    v_blocks = vd // LANES
    width = k_new.shape[2]
    n_heads = width // vd
    n_pages = page_table.shape[1]
    page = cache_v.shape[1] // n_heads
    tp = rows // (2 * n_heads)
    g = math.gcd(pages_per_step, n_pages)

    row_spec = pl.BlockSpec((1, tp, width), lambda b, s, pt: (b, 0, 0))
    dense_spec = pl.BlockSpec((1, g * page, width), lambda b, s, pt: (b, s, 0))
    dense_shape = jax.ShapeDtypeStruct((db, n_pages * page, width), _BF16)
    grid_spec = pltpu.PrefetchScalarGridSpec(
        num_scalar_prefetch=1,
        grid=(db, n_pages // g),
        in_specs=[
            pl.BlockSpec((4, hd), lambda b, s, pt: (0, 0)),
            pl.BlockSpec((1, rows, hd), lambda b, s, pt: (b, 0, 0)),
            row_spec, row_spec, row_spec,
            pl.BlockSpec((1, vd), lambda b, s, pt: (0, 0)),
            pl.BlockSpec(memory_space=pl.ANY),
            pl.BlockSpec(memory_space=pl.ANY),
        ],
        out_specs=[row_spec, dense_spec, dense_spec],
        scratch_shapes=[
            pltpu.VMEM((PAGE_RING_SLOTS * g, cache_k.shape[1], LANES), _F32),
            pltpu.VMEM((PAGE_RING_SLOTS * g * v_blocks, cache_v.shape[1], LANES), _F32),
            pltpu.SemaphoreType.DMA((PAGE_RING_SLOTS,)),
            *_decode_scratch(rows, vd),
        ],
    )
    return pl.pallas_call(
        functools.partial(_attn_decode_paged_kernel, pages_per_step=g, n_heads=n_heads, t_real=t_real,
                          tp=tp, scale=hd ** -0.5, lam_init=lam_init),
        grid_spec=grid_spec,
        out_shape=[jax.ShapeDtypeStruct((db, tp, width), _BF16), dense_shape, dense_shape],
        compiler_params=_params("arbitrary", "arbitrary"),
        name="attn_decode_paged",
    )(page_table.reshape(-1), lam, q, k_new, v_new, z, subln.reshape(1, vd), cache_k, cache_v)


def _attn_decode_dense(q, k_past, v_past, k_new, v_new, z, lam, subln, lam_init, *, t_real, tk):
    db, rows, hd = q.shape
    vd = 2 * hd
    past, width = k_past.shape[1:]
    n_heads = width // vd
    tp = rows // (2 * n_heads)
    tk = math.gcd(tk, past)
    row_spec = pl.BlockSpec((1, tp, width), lambda b, s: (b, 0, 0))
    dense_spec = pl.BlockSpec((1, tk, width), lambda b, s: (b, s, 0))
    return pl.pallas_call(
        functools.partial(_attn_decode_dense_kernel, n_heads=n_heads, t_real=t_real, tp=tp,
                          scale=hd ** -0.5, lam_init=lam_init),
        grid=(db, past // tk),
        in_specs=[
            pl.BlockSpec((4, hd), lambda b, s: (0, 0)),
            pl.BlockSpec((1, rows, hd), lambda b, s: (b, 0, 0)),
            row_spec, row_spec, row_spec,
            pl.BlockSpec((1, vd), lambda b, s: (0, 0)),
            dense_spec, dense_spec,
        ],
        out_specs=row_spec,
        out_shape=jax.ShapeDtypeStruct((db, tp, width), _BF16),
        scratch_shapes=_decode_scratch(rows, vd),
        compiler_params=_params("arbitrary", "arbitrary"),
        name="attn_decode_dense",
    )(lam, q, k_new, v_new, z, subln.reshape(1, vd), k_past, v_past)


def _out_proj_kernel(x_ref, g_ref, w_ref, y_ref):
    y_ref[...] = x_ref[...] + _dot(g_ref[...], w_ref[...])


def _out_proj(x, g, w, *, layer, tm):
    m, d = x.shape
    a = g.shape[1]
    tm = min(tm, m)
    return pl.pallas_call(
        _out_proj_kernel,
        grid=(m // tm,),
        in_specs=[pl.BlockSpec((tm, d), lambda i: (i, 0)),
                  pl.BlockSpec((tm, a), lambda i: (i, 0)),
                  pl.BlockSpec((None, a, d), lambda i: (layer, 0, 0))],
        out_specs=pl.BlockSpec((tm, d), lambda i: (i, 0)),
        out_shape=jax.ShapeDtypeStruct((m, d), _F32),
        compiler_params=_params("arbitrary"),
        name="out_proj",
    )(x, g, w)


def _rope_tables(pos, hd, out_scale=1.0):
    rot = hd // 4
    half = rot // 2
    inv = ROPE_THETA ** (-jnp.arange(0, rot, 2, dtype=_F32) / rot)
    ang = pos.astype(_F32)[:, None] * inv[None, :]
    cos, sin = jnp.cos(ang), jnp.sin(ang)
    n = pos.shape[0]
    cos_t = jnp.concatenate([cos, cos, jnp.ones((n, hd - rot), _F32)], axis=1)
    sa_t = jnp.concatenate([-sin, jnp.zeros((n, hd - half), _F32)], axis=1)
    sb_t = jnp.concatenate([jnp.zeros((n, half), _F32), sin, jnp.zeros((n, hd - rot), _F32)], axis=1)
    return cos_t * out_scale, sa_t * out_scale, sb_t * out_scale


def _pad_rows(x, tp):
    return jnp.pad(x, ((0, 0), (0, tp - x.shape[1]), (0, 0)))


def kernel(x_prompt, x_sample, state_conv, cache_k, cache_v, page_table, norm_a, w_in_a, conv_w, w_out_a, norm_kv, w_kv, k_norm, norm_b, w_in_b, q_norm, lam, subln_w, w_out_b):
    bsz, seq, d = x_prompt.shape
    db, t_dec, _ = x_sample.shape
    n_a, n_b = norm_a.shape[0], norm_b.shape[0]
    n_phys, page, n_heads, _, hd = cache_k.shape
    n_pages = page_table.shape[1]
    past = n_pages * page
    width = n_heads * 2 * hd
    tp = -(-t_dec // SUBLANES) * SUBLANES
    tm = 512

    w_in_a16, w_out_a16 = w_in_a.astype(_BF16), w_out_a.astype(_BF16)
    w_kv16, w_in_b16, w_out_b16 = w_kv.astype(_BF16), w_in_b.astype(_BF16), w_out_b.astype(_BF16)
    pos_s = jnp.tile(past + jnp.arange(t_dec), db)
    tab_p = _rope_tables(jnp.arange(seq), hd)
    tab_s = _rope_tables(pos_s, hd)
    tab_pq = _rope_tables(jnp.arange(seq), hd, hd ** -0.5 * LOG2_E)
    cache_k3 = cache_k.reshape(n_phys, page * n_heads * 2, hd)
    cache_v3 = cache_v.reshape(n_phys, page * n_heads, 2 * hd)

    xp = x_prompt.reshape(bsz * seq, d)
    xs = x_sample.reshape(db * t_dec, d)
    tiles_per_seq = max(seq // tm, 1)
    conv_p, conv_s = [], []
    for layer in range(n_a):
        xp, up = _conv_layer(xp, norm_a[layer], w_in_a16, conv_w, w_out_a16, layer=layer,
                             seq_rows=seq, tm=tm, tc=512)
        c_dim = up.shape[-1]
        conv_p.append(up.reshape(bsz, tiles_per_seq, SUBLANES, c_dim)[:, -1, SUBLANES - 2:, :])
        prev = state_conv[layer]
        zero = jnp.zeros_like(prev[:, :1])
        p1 = jnp.concatenate([prev[:, 1:2]] + [zero] * (t_dec - 1), axis=1)
        p2 = jnp.concatenate([prev[:, 0:1], prev[:, 1:2]] + [zero] * (t_dec - 2), axis=1)
        xs, us = _conv_layer(xs, norm_a[layer], w_in_a16, conv_w, w_out_a16, layer=layer,
                             seq_rows=t_dec, prev=(p1.reshape(db * t_dec, c_dim), p2.reshape(db * t_dec, c_dim)),
                             tm=tm, tc=512)
        conv_s.append(us.reshape(db, t_dec, c_dim)[:, t_dec - 2:, :])

    proj = functools.partial(_proj, n=width, tm=tm)
    w_kv16 = w_kv16[None]
    kp32, kp16 = proj(xp, norm_kv, w_kv16, layer=0, col_block=0, out_kinds=("f32_rows", "bf16"), name="k_prompt",
                      head_gain=k_norm, tables=tab_p, seq_rows=seq)
    vp32, vp16 = proj(xp, norm_kv, w_kv16, layer=0, col_block=1, out_kinds=("f32", "bf16"), name="v_prompt")
    ks32, = proj(xs, norm_kv, w_kv16, layer=0, col_block=0, out_kinds=("f32",), name="k_sample",
                 head_gain=k_norm, tables=tab_s, seq_rows=db * t_dec)
    vs32, = proj(xs, norm_kv, w_kv16, layer=0, col_block=1, out_kinds=("f32",), name="v_sample")
    k_new = _pad_rows(ks32.reshape(db, t_dec, width), tp)
    v_new = _pad_rows(vs32.reshape(db, t_dec, width), tp)

    for j in range(n_b):
        lam_init = 0.8 - 0.6 * math.exp(-0.3 * (n_a + j))
        q16, = proj(xp, norm_b[j], w_in_b16, layer=j, col_block=0, out_kinds=("bf16",), name="q_prompt",
                    head_gain=q_norm[j], tables=tab_pq, seq_rows=seq)
        z16, = proj(xp, norm_b[j], w_in_b16, layer=j, col_block=1, out_kinds=("bf16",), name="z_prompt")
        gp = _attn_prompt(q16.reshape(bsz, seq, width), kp16.reshape(bsz, seq, width),
                          vp16.reshape(bsz, seq, width), z16.reshape(bsz, seq, width),
                          lam[j], subln_w[j], lam_init, hd=hd, tq=512)
        xp = _out_proj(xp, gp.reshape(bsz * seq, width), w_out_b16, layer=j, tm=tm)

        qs32, = proj(xs, norm_b[j], w_in_b16, layer=j, col_block=0, out_kinds=("f32",), name="q_sample",
                     head_gain=q_norm[j], tables=tab_s, seq_rows=db * t_dec)
        zs32, = proj(xs, norm_b[j], w_in_b16, layer=j, col_block=1, out_kinds=("f32",), name="z_sample")
        qd = _pad_rows(qs32.reshape(db, t_dec, width), tp).reshape(db, tp, 2 * n_heads, hd)
        qd = qd.transpose(0, 2, 1, 3).reshape(db, 2 * n_heads * tp, hd)
        zd = _pad_rows(zs32.reshape(db, t_dec, width), tp)
        if j == 0:
            gs, k_past16, v_past16 = _attn_decode_paged(qd, cache_k3, cache_v3, page_table, k_new, v_new, zd,
                                                        lam[j], subln_w[j], lam_init, t_real=t_dec,
                                                        pages_per_step=4)
        else:
            gs = _attn_decode_dense(qd, k_past16, v_past16, k_new, v_new, zd, lam[j], subln_w[j], lam_init,
                                    t_real=t_dec, tk=2048)
        xs = _out_proj(xs, gs[:, :t_dec].reshape(db * t_dec, width), w_out_b16, layer=j, tm=tm)

    return (xp.reshape(bsz, seq, d), xs.reshape(db, t_dec, d),
            kp32.reshape(bsz, seq, n_heads, 2, hd), vp32.reshape(bsz, seq, n_heads, 2 * hd),
            jnp.stack(conv_p),
            ks32.reshape(db, t_dec, n_heads, 2, hd), vs32.reshape(db, t_dec, n_heads, 2 * hd),
            jnp.stack(conv_s))
```

```python
import functools
import math

import jax
import jax.numpy as jnp
from jax import lax
from jax.experimental import pallas as pl
from jax.experimental.pallas import tpu as pltpu

EPS = 1e-6
ROPE_THETA = 500000.0
LANES = 128
SUBLANES = 8
MXU_WIDTH = 256
VMEM_LIMIT_BYTES = 56 * 1024 * 1024
LOG2_E = math.log2(math.e)

_F32 = jnp.float32
_BF16 = jnp.bfloat16


def _dot(a, b):
    return jnp.dot(a, b, preferred_element_type=_F32)


def _dot_nt(a, b):
    return lax.dot_general(a, b, (((1,), (1,)), ((), ())), preferred_element_type=_F32)


def _silu(z):
    return z * (1.0 / (1.0 + jnp.exp(-z)))


def _rms_scale(x):
    return lax.rsqrt(jnp.mean(x * x, axis=-1, keepdims=True) + EPS)


def _params(*sem):
    return pltpu.CompilerParams(dimension_semantics=sem, vmem_limit_bytes=VMEM_LIMIT_BYTES)


def _conv_layer_kernel(*refs, prompt, tiles_per_seq, seq_rows):
    if prompt:
        (x_ref, g_ref, wb_ref, wc_ref, wh_ref, wz_ref, cw_ref, wo_ref,
         y_ref, u_ref, xn_scr, carry_scr) = refs
    else:
        (x_ref, g_ref, wb_ref, wc_ref, wh_ref, wz_ref, cw_ref, wo_ref, p1_ref, p2_ref,
         y_ref, u_ref, xn_scr) = refs
    i = pl.program_id(0)
    j = pl.program_id(1)

    @pl.when(j == 0)
    def _():
        x = x_ref[...]
        xn_scr[...] = (x * _rms_scale(x) * g_ref[...]).astype(_BF16)
        y_ref[...] = x

    xn = xn_scr[...]
    tm = xn.shape[0]
    gated = []
    for sub in range(wb_ref.shape[1] // MXU_WIDTH):
        cols = slice(sub * MXU_WIDTH, (sub + 1) * MXU_WIDTH)
        b = _dot(xn, wb_ref[:, cols])
        c = _dot(xn, wc_ref[:, cols])
        h = _dot(xn, wh_ref[:, cols])
        z = _dot(xn, wz_ref[:, cols])
        u = c * h
        row = lax.broadcasted_iota(jnp.int32, u.shape, 0)
        if prompt:
            first = (i % tiles_per_seq) == 0
            cr = carry_scr[j, :, cols]
            c0 = jnp.where(first, 0.0, cr[SUBLANES - 2:SUBLANES - 1, :])
            c1 = jnp.where(first, 0.0, cr[SUBLANES - 1:SUBLANES, :])
            um1 = jnp.where(row == 0, c1, pltpu.roll(u, 1, 0))
            um2 = jnp.where(row == 0, c0, jnp.where(row == 1, c1, pltpu.roll(u, 2, 0)))
            tail = u[tm - SUBLANES:tm, :]
            carry_scr[j, :, cols] = tail
            u_ref[0, :, cols] = tail
        else:
            t = row % seq_rows
            um1 = jnp.where(t == 0, p1_ref[:, cols], pltpu.roll(u, 1, 0))
            um2 = jnp.where(t <= 1, p2_ref[:, cols], pltpu.roll(u, 2, 0))
            u_ref[:, cols] = u
        conv = cw_ref[0:1, cols] * um2 + cw_ref[1:2, cols] * um1 + cw_ref[2:3, cols] * u
        gated.append((b * conv * _silu(z)).astype(_BF16))
    y_ref[...] += _dot(jnp.concatenate(gated, axis=1), wo_ref[...])


def _conv_layer(x, norm_g, w_in, conv_w, w_out, *, layer, seq_rows, prev=None, tm, tc):
    m, d = x.shape
    c = conv_w.shape[2]
    prompt = prev is None
    tm = min(tm, m)
    tc = min(tc, c)
    ni, nj = m // tm, c // tc
    tiles_per_seq = max(seq_rows // tm, 1)

    def w_in_spec(gate):
        return pl.BlockSpec((None, d, tc), lambda i, j, gate=gate: (layer, 0, gate * nj + j))

    in_specs = [
        pl.BlockSpec((tm, d), lambda i, j: (i, 0)),
        pl.BlockSpec((1, d), lambda i, j: (0, 0)),
        w_in_spec(0), w_in_spec(1), w_in_spec(2), w_in_spec(3),
        pl.BlockSpec((None, 3, tc), lambda i, j: (layer, 0, j)),
        pl.BlockSpec((None, tc, d), lambda i, j: (layer, j, 0)),
    ]
    args = [x, norm_g.reshape(1, d), w_in, w_in, w_in, w_in, conv_w, w_out]
    scratch = [pltpu.VMEM((tm, d), _BF16)]
    if prompt:
        u_shape = jax.ShapeDtypeStruct((ni, SUBLANES, c), _F32)
        u_spec = pl.BlockSpec((1, SUBLANES, tc), lambda i, j: (i, 0, j))
        scratch.append(pltpu.VMEM((nj, SUBLANES, tc), _F32))
    else:
        in_specs += [pl.BlockSpec((tm, tc), lambda i, j: (i, j))] * 2
        args += list(prev)
        u_shape = jax.ShapeDtypeStruct((m, c), _F32)
        u_spec = pl.BlockSpec((tm, tc), lambda i, j: (i, j))
    return pl.pallas_call(
        functools.partial(_conv_layer_kernel, prompt=prompt, tiles_per_seq=tiles_per_seq,
                          seq_rows=seq_rows),
        grid=(ni, nj),
        in_specs=in_specs,
        out_specs=[pl.BlockSpec((tm, d), lambda i, j: (i, 0)), u_spec],
        out_shape=[jax.ShapeDtypeStruct((m, d), _F32), u_shape],
        scratch_shapes=scratch,
        compiler_params=_params("arbitrary", "arbitrary"),
        name="conv_layer_prompt" if prompt else "conv_layer_sample",
    )(*args)


def _proj_heads_kernel(x_ref, g_ref, w_ref, hg_ref, cos_ref, sa_ref, sb_ref, *refs, out_kinds):
    outs = dict(zip(out_kinds, refs[:-1]))
    h_scr = refs[-1]
    i = pl.program_id(0)
    tm = x_ref.shape[0]

    @pl.when(i == 0)
    def _():
        h_scr[...] = jnp.zeros(h_scr.shape, _F32)

    cos, sa, sb, hg = cos_ref[...], sa_ref[...], sb_ref[...], hg_ref[...]
    n_groups = w_ref.shape[1] // LANES
    prev = (i + 1) % 2
    for grp in range(n_groups):
        sl = slice(grp * LANES, (grp + 1) * LANES)
        xg = h_scr[prev, :, sl]
        yn = xg * _rms_scale(xg) * hg
        y = yn * cos + pltpu.roll(yn, LANES - 16, 1) * sa + pltpu.roll(yn, 16, 1) * sb
        if "f32" in outs:
            outs["f32"][:, sl] = y
        if "f32_rows" in outs:
            outs["f32_rows"][pl.ds(grp, tm, stride=n_groups), :] = y
        if "bf16" in outs:
            outs["bf16"][:, sl] = y.astype(_BF16)

    x = x_ref[...]
    xn = (x * _rms_scale(x) * g_ref[...]).astype(_BF16)
    h_scr[i % 2] = _dot(xn, w_ref[...])


def _proj_plain_kernel(x_ref, g_ref, w_ref, *out_refs, out_kinds):
    outs = dict(zip(out_kinds, out_refs))
    x = x_ref[...]
    xn = (x * _rms_scale(x) * g_ref[...]).astype(_BF16)
    h = _dot(xn, w_ref[...])
    if "f32" in outs:
        outs["f32"][...] = h
    if "bf16" in outs:
        outs["bf16"][...] = h.astype(_BF16)


def _proj(x, norm_g, w, *, layer, n, col_block, out_kinds, tm, name, head_gain=None, tables=None,
          seq_rows=None):
    m, d = x.shape
    tm = min(tm, m)
    n_tiles = m // tm
    heads = head_gain is not None
    n_steps = n_tiles + 1 if heads else n_tiles
    in_tile = (lambda i: jnp.minimum(i, n_tiles - 1)) if heads else (lambda i: i)
    out_tile = (lambda i: jnp.maximum(i - 1, 0)) if heads else (lambda i: i)
    row_spec = pl.BlockSpec((tm, n), lambda i: (out_tile(i), 0))
    in_specs = [pl.BlockSpec((tm, d), lambda i: (in_tile(i), 0)),
                pl.BlockSpec((1, d), lambda i: (0, 0)),
                pl.BlockSpec((None, d, n), lambda i: (layer, 0, col_block))]
    args = [x, norm_g.reshape(1, d), w]
    scratch = []
    if heads:
        body = _proj_heads_kernel
        tiles_per_seq = max(seq_rows // tm, 1)
        tab_spec = pl.BlockSpec((tm, LANES), lambda i: (out_tile(i) % tiles_per_seq, 0))
        in_specs += [pl.BlockSpec((1, LANES), lambda i: (0, 0)), tab_spec, tab_spec, tab_spec]
        args += [head_gain.reshape(1, LANES), *tables]
        scratch = [pltpu.VMEM((2, tm, n), _F32)]
    else:
        body = _proj_plain_kernel
    groups = n // LANES
    out_specs = [pl.BlockSpec((tm * groups, LANES), lambda i: (out_tile(i), 0)) if kind == "f32_rows"
                 else row_spec for kind in out_kinds]
    out_shape = [jax.ShapeDtypeStruct((m * groups, LANES) if kind == "f32_rows" else (m, n),
                                      _BF16 if kind == "bf16" else _F32) for kind in out_kinds]
    return pl.pallas_call(
        functools.partial(body, out_kinds=tuple(out_kinds)),
        grid=(n_steps,),
        in_specs=in_specs,
        out_specs=out_specs,
        out_shape=out_shape,
        scratch_shapes=scratch,
        compiler_params=_params("arbitrary"),
        name=name,
    )(*args)


def _lambda_full(lam_ref, lam_init):
    lf = lam_ref[...]
    t1 = jnp.sum(lf[0:1, :] * lf[1:2, :], axis=-1, keepdims=True)
    t2 = jnp.sum(lf[2:3, :] * lf[3:4, :], axis=-1, keepdims=True)
    return jnp.exp(t1) - jnp.exp(t2) + lam_init


def _head_output(o1, o2, lam_full, z, subln, lam_init):
    o = o1 - lam_full * o2
    on = o * _rms_scale(o) * subln * (1.0 - lam_init)
    return on * _silu(z)


def _lane_repeat(x, n):
    return jnp.concatenate([x] * n, axis=1)


def _attn_prompt_kernel(lam_ref, q_ref, k_ref, v_ref, z_ref, sg_ref, o_ref,
                        kt_scr, s_scr, m_scr, l_scr, acc_scr, *, tq, hd, lam_init):
    qi = pl.program_id(2)
    n_kt = kt_scr.shape[0]
    vd = 2 * hd

    @pl.when(qi == 0)
    def _():
        for t in range(n_kt):
            kt_scr[t] = k_ref[0, t * tq:(t + 1) * tq, :].T

    q = q_ref[0]
    m_scr[...] = jnp.full(m_scr.shape, -jnp.inf, _F32)
    l_scr[...] = jnp.zeros(l_scr.shape, _F32)
    acc_scr[...] = jnp.zeros(acc_scr.shape, _F32)

    def scores(kt, slot):
        for c in range(2):
            s_scr[slot, c] = _dot(q[:, c * hd:(c + 1) * hd], kt_scr[kt, c * hd:(c + 1) * hd, :])

    def consume(kt, slot, masked):
        start = pl.multiple_of(kt * tq, tq)
        vs = v_ref[0, pl.ds(start, tq), :]
        for c in range(2):
            s = s_scr[slot, c]
            if masked:
                row = lax.broadcasted_iota(jnp.int32, s.shape, 0)
                col = lax.broadcasted_iota(jnp.int32, s.shape, 1)
                s = jnp.where(col <= row, s, -jnp.inf)
            m_prev = m_scr[c]
            m_new = jnp.maximum(m_prev, jnp.max(s, axis=-1, keepdims=True))
            alpha = jnp.exp2(m_prev - m_new)
            p = jnp.exp2(s - _lane_repeat(m_new, tq // LANES))
            l_scr[c] = alpha * l_scr[c] + jnp.sum(p, axis=-1, keepdims=True)
            acc_scr[c] = _lane_repeat(alpha, vd // LANES) * acc_scr[c] + _dot(p.astype(_BF16), vs)
            m_scr[c] = m_new

    scores(0, 0)

    def pair(i, carry):
        kt = 2 * i
        scores(kt + 1, 1)
        consume(kt, 0, False)
        scores(kt + 2, 0)
        consume(kt + 1, 1, False)
        return carry

    lax.fori_loop(0, qi // 2, pair, 0)

    @pl.when(qi % 2 == 0)
    def _():
        consume(qi, 0, True)

    @pl.when(qi % 2 == 1)
    def _():
        scores(qi, 1)
        consume(qi - 1, 0, False)
        consume(qi, 1, True)

    o1 = acc_scr[0] * _lane_repeat(1.0 / l_scr[0], vd // LANES)
    o2 = acc_scr[1] * _lane_repeat(1.0 / l_scr[1], vd // LANES)
    out = _head_output(o1, o2, _lambda_full(lam_ref, lam_init), z_ref[0].astype(_F32),
                       sg_ref[...], lam_init)
    o_ref[0] = out.astype(o_ref.dtype)


def _attn_prompt(q, k, v, z, lam, subln, lam_init, *, hd, tq):
    b, s, width = q.shape
    vd = 2 * hd
    n_heads = width // vd
    tq = min(tq, s)
    q_spec = pl.BlockSpec((1, tq, vd), lambda bi, h, qi: (bi, qi, h))
    kv_spec = pl.BlockSpec((1, s, vd), lambda bi, h, qi: (bi, 0, h))
    return pl.pallas_call(
        functools.partial(_attn_prompt_kernel, tq=tq, hd=hd, lam_init=lam_init),
        grid=(b, n_heads, s // tq),
        in_specs=[
            pl.BlockSpec((4, hd), lambda bi, h, qi: (0, 0)),
            q_spec, kv_spec, kv_spec, q_spec,
            pl.BlockSpec((1, vd), lambda bi, h, qi: (0, 0)),
        ],
        out_specs=q_spec,
        out_shape=jax.ShapeDtypeStruct((b, s, width), _BF16),
        scratch_shapes=[pltpu.VMEM((s // tq, vd, tq), _BF16), pltpu.VMEM((2, 2, tq, tq), _F32),
                        pltpu.VMEM((2, tq, LANES), _F32), pltpu.VMEM((2, tq, LANES), _F32),
                        pltpu.VMEM((2, tq, vd), _F32)],
        compiler_params=_params("arbitrary", "arbitrary", "arbitrary"),
        name="attn_prompt",
    )(lam, q, k, v, z, subln.reshape(1, vd))


def _decode_steps(cached_k, cached_v, lam_ref, q_ref, kn_ref, vn_ref, z_ref, sg_ref, o_ref,
                  m_scr, l_scr, acc_scr, *, n_heads, t_real, tp, scale, lam_init):
    hd = q_ref.shape[2]
    vd = 2 * hd
    step = pl.program_id(1)

    @pl.when(step == 0)
    def _():
        m_scr[...] = jnp.full(m_scr.shape, -jnp.inf, _F32)
        l_scr[...] = jnp.zeros(l_scr.shape, _F32)
        acc_scr[...] = jnp.zeros(acc_scr.shape, _F32)

    q = q_ref[0].astype(_BF16)

    def update(k_of_group, v_of_head, mask=None):
        s = jnp.concatenate([_dot_nt(q[g * tp:(g + 1) * tp, :], k_of_group(g)) for g in range(2 * n_heads)],
                            axis=0)
        if mask is not None:
            s = jnp.where(mask(s.shape), s, -jnp.inf)
        m_prev = m_scr[...]
        m_new = jnp.maximum(m_prev, jnp.max(s, axis=-1, keepdims=True))
        alpha = jnp.exp((m_prev - m_new) * scale)
        p = jnp.exp((s - m_new) * scale)
        l_scr[...] = alpha * l_scr[...] + jnp.sum(p, axis=-1, keepdims=True)
        p = p.astype(_BF16)
        pv = jnp.concatenate([_dot(p[h * 2 * tp:(h + 1) * 2 * tp, :], v_of_head(h)) for h in range(n_heads)],
                             axis=0)
        acc_scr[...] = alpha * acc_scr[...] + pv
        m_scr[...] = m_new

    update(cached_k, cached_v)

    @pl.when(step == pl.num_programs(1) - 1)
    def _():
        def new_key_mask(shape):
            t = lax.broadcasted_iota(jnp.int32, shape, 0) % tp
            jj = lax.broadcasted_iota(jnp.int32, shape, 1)
            return (jj <= t) & (jj < t_real)

        update(lambda g: kn_ref[0, :, g * hd:(g + 1) * hd].astype(_BF16),
               lambda h: vn_ref[0, :, h * vd:(h + 1) * vd].astype(_BF16), new_key_mask)
        lam_full = _lambda_full(lam_ref, lam_init)
        acc = acc_scr[...] * (1.0 / l_scr[...])
        for h in range(n_heads):
            r1 = (2 * h) * tp
            cols = slice(h * vd, (h + 1) * vd)
            out = _head_output(acc[r1:r1 + tp, :], acc[r1 + tp:r1 + 2 * tp, :], lam_full,
                               z_ref[0, :, cols], sg_ref[...], lam_init)
            o_ref[0, :, cols] = out.astype(o_ref.dtype)


PAGE_RING_SLOTS = 3


def _page_copies(pt_ref, k_hbm, v_hbm, kbuf, vbuf, sem, step, slot, *, pages_per_step):
    v_blocks = v_hbm.shape[2] // LANES
    copies = []
    for g in range(pages_per_step):
        pid = pt_ref[step * pages_per_step + g]
        i = slot * pages_per_step + g
        copies.append(pltpu.make_async_copy(k_hbm.at[pid], kbuf.at[i], sem.at[slot]))
        for cb in range(v_blocks):
            copies.append(pltpu.make_async_copy(v_hbm.at[pid, :, pl.ds(cb * LANES, LANES)],
                                                vbuf.at[i * v_blocks + cb], sem.at[slot]))
    return copies


def _attn_decode_paged_kernel(pt_ref, lam_ref, q_ref, kn_ref, vn_ref, z_ref, sg_ref, k_hbm, v_hbm,
                              o_ref, kd_ref, vd_ref, kbuf, vbuf, sem, m_scr, l_scr, acc_scr, *,
                              pages_per_step, n_heads, **kw):
    hd = q_ref.shape[2]
    vd = 2 * hd
    v_blocks = vd // LANES
    k_rows = 2 * n_heads
    page = k_hbm.shape[1] // k_rows
    n_steps = pl.num_programs(1)
    total = pl.num_programs(0) * n_steps
    lin = pl.program_id(0) * n_steps + pl.program_id(1)
    slot = lin % PAGE_RING_SLOTS
    copies = functools.partial(_page_copies, pt_ref, k_hbm, v_hbm, kbuf, vbuf, sem,
                               pages_per_step=pages_per_step)

    @pl.when(lin == 0)
    def _():
        for ahead in range(PAGE_RING_SLOTS - 1):
            @pl.when(ahead < total)
            def _():
                for cp in copies(ahead, ahead):
                    cp.start()

    @pl.when(lin + PAGE_RING_SLOTS - 1 < total)
    def _():
        nxt = lin + PAGE_RING_SLOTS - 1
        for cp in copies(nxt, nxt % PAGE_RING_SLOTS):
            cp.start()

    for cp in copies(lin, slot):
        cp.wait()

    def cached_k(g):
        k = jnp.concatenate([kbuf[slot * pages_per_step + pg, pl.ds(g, page, stride=k_rows), :]
                             for pg in range(pages_per_step)], axis=0).astype(_BF16)
        kd_ref[0, :, g * hd:(g + 1) * hd] = k
        return k

    def cached_v(h):
        v = jnp.concatenate(
            [jnp.concatenate([vbuf[(slot * pages_per_step + pg) * v_blocks + cb,
                                   pl.ds(h, page, stride=n_heads), :] for cb in range(v_blocks)], axis=1)
             for pg in range(pages_per_step)], axis=0).astype(_BF16)
        vd_ref[0, :, h * vd:(h + 1) * vd] = v
        return v

    _decode_steps(cached_k, cached_v, lam_ref, q_ref, kn_ref, vn_ref, z_ref, sg_ref, o_ref,
                  m_scr, l_scr, acc_scr, n_heads=n_heads, **kw)


def _attn_decode_dense_kernel(lam_ref, q_ref, kn_ref, vn_ref, z_ref, sg_ref, kd_ref, vd_ref, o_ref,
                              m_scr, l_scr, acc_scr, **kw):
    hd = q_ref.shape[2]
    vd = 2 * hd
    _decode_steps(lambda g: kd_ref[0, :, g * hd:(g + 1) * hd], lambda h: vd_ref[0, :, h * vd:(h + 1) * vd],
                  lam_ref, q_ref, kn_ref, vn_ref, z_ref, sg_ref, o_ref, m_scr, l_scr, acc_scr, **kw)


def _decode_scratch(rows, vd):
    return [pltpu.VMEM((rows, 1), _F32), pltpu.VMEM((rows, 1), _F32), pltpu.VMEM((rows, vd), _F32)]


def _attn_decode_paged(q, cache_k, cache_v, page_table, k_new, v_new, z, lam, subln, lam_init, *,
                       t_real, pages_per_step):
    db, rows, hd = q.shape
    vd = 2 * hd
    v_blocks = vd // LANES
    width = k_new.shape[2]
    n_heads = width // vd
    n_pages = page_table.shape[1]
    page = cache_v.shape[1] // n_heads
    tp = rows // (2 * n_heads)
    g = math.gcd(pages_per_step, n_pages)

    row_spec = pl.BlockSpec((1, tp, width), lambda b, s, pt: (b, 0, 0))
    dense_spec = pl.BlockSpec((1, g * page, width), lambda b, s, pt: (b, s, 0))
    dense_shape = jax.ShapeDtypeStruct((db, n_pages * page, width), _BF16)
    grid_spec = pltpu.PrefetchScalarGridSpec(
        num_scalar_prefetch=1,
        grid=(db, n_pages // g),
        in_specs=[
            pl.BlockSpec((4, hd), lambda b, s, pt: (0, 0)),
            pl.BlockSpec((1, rows, hd), lambda b, s, pt: (b, 0, 0)),
            row_spec, row_spec, row_spec,
            pl.BlockSpec((1, vd), lambda b, s, pt: (0, 0)),
            pl.BlockSpec(memory_space=pl.ANY),
            pl.BlockSpec(memory_space=pl.ANY),
        ],
        out_specs=[row_spec, dense_spec, dense_spec],
        scratch_shapes=[
            pltpu.VMEM((PAGE_RING_SLOTS * g, cache_k.shape[1], LANES), _F32),
            pltpu.VMEM((PAGE_RING_SLOTS * g * v_blocks, cache_v.shape[1], LANES), _F32),
            pltpu.SemaphoreType.DMA((PAGE_RING_SLOTS,)),
            *_decode_scratch(rows, vd),
        ],
    )
    return pl.pallas_call(
        functools.partial(_attn_decode_paged_kernel, pages_per_step=g, n_heads=n_heads, t_real=t_real,
                          tp=tp, scale=hd ** -0.5, lam_init=lam_init),
        grid_spec=grid_spec,
        out_shape=[jax.ShapeDtypeStruct((db, tp, width), _BF16), dense_shape, dense_shape],
        compiler_params=_params("arbitrary", "arbitrary"),
        name="attn_decode_paged",
    )(page_table.reshape(-1), lam, q, k_new, v_new, z, subln.reshape(1, vd), cache_k, cache_v)


def _attn_decode_dense(q, k_past, v_past, k_new, v_new, z, lam, subln, lam_init, *, t_real, tk):
    db, rows, hd = q.shape
    vd = 2 * hd
    past, width = k_past.shape[1:]
    n_heads = width // vd
    tp = rows // (2 * n_heads)
    tk = math.gcd(tk, past)
    row_spec = pl.BlockSpec((1, tp, width), lambda b, s: (b, 0, 0))
    dense_spec = pl.BlockSpec((1, tk, width), lambda b, s: (b, s, 0))
    return pl.pallas_call(
        functools.partial(_attn_decode_dense_kernel, n_heads=n_heads, t_real=t_real, tp=tp,
                          scale=hd ** -0.5, lam_init=lam_init),
        grid=(db, past // tk),
        in_specs=[
            pl.BlockSpec((4, hd), lambda b, s: (0, 0)),
            pl.BlockSpec((1, rows, hd), lambda b, s: (b, 0, 0)),
            row_spec, row_spec, row_spec,
            pl.BlockSpec((1, vd), lambda b, s: (0, 0)),
            dense_spec, dense_spec,
        ],
        out_specs=row_spec,
        out_shape=jax.ShapeDtypeStruct((db, tp, width), _BF16),
        scratch_shapes=_decode_scratch(rows, vd),
        compiler_params=_params("arbitrary", "arbitrary"),
        name="attn_decode_dense",
    )(lam, q, k_new, v_new, z, subln.reshape(1, vd), k_past, v_past)


def _out_proj_kernel(x_ref, g_ref, w_ref, y_ref):
    y_ref[...] = x_ref[...] + _dot(g_ref[...], w_ref[...])


def _out_proj(x, g, w, *, layer, tm):
    m, d = x.shape
    a = g.shape[1]
    tm = min(tm, m)
    return pl.pallas_call(
        _out_proj_kernel,
        grid=(m // tm,),
        in_specs=[pl.BlockSpec((tm, d), lambda i: (i, 0)),
                  pl.BlockSpec((tm, a), lambda i: (i, 0)),
                  pl.BlockSpec((None, a, d), lambda i: (layer, 0, 0))],
        out_specs=pl.BlockSpec((tm, d), lambda i: (i, 0)),
        out_shape=jax.ShapeDtypeStruct((m, d), _F32),
        compiler_params=_params("arbitrary"),
        name="out_proj",
    )(x, g, w)


def _rope_tables(pos, hd, out_scale=1.0):
    rot = hd // 4
    half = rot // 2
    inv = ROPE_THETA ** (-jnp.arange(0, rot, 2, dtype=_F32) / rot)
    ang = pos.astype(_F32)[:, None] * inv[None, :]
    cos, sin = jnp.cos(ang), jnp.sin(ang)
    n = pos.shape[0]
    cos_t = jnp.concatenate([cos, cos, jnp.ones((n, hd - rot), _F32)], axis=1)
    sa_t = jnp.concatenate([-sin, jnp.zeros((n, hd - half), _F32)], axis=1)
    sb_t = jnp.concatenate([jnp.zeros((n, half), _F32), sin, jnp.zeros((n, hd - rot), _F32)], axis=1)
    return cos_t * out_scale, sa_t * out_scale, sb_t * out_scale


def _pad_rows(x, tp):
    return jnp.pad(x, ((0, 0), (0, tp - x.shape[1]), (0, 0)))


def kernel(x_prompt, x_sample, state_conv, cache_k, cache_v, page_table, norm_a, w_in_a, conv_w, w_out_a, norm_kv, w_kv, k_norm, norm_b, w_in_b, q_norm, lam, subln_w, w_out_b):
    bsz, seq, d = x_prompt.shape
    db, t_dec, _ = x_sample.shape
    n_a, n_b = norm_a.shape[0], norm_b.shape[0]
    n_phys, page, n_heads, _, hd = cache_k.shape
    n_pages = page_table.shape[1]
    past = n_pages * page
    width = n_heads * 2 * hd
    tp = -(-t_dec // SUBLANES) * SUBLANES
    tm = 512

    w_in_a16, w_out_a16 = w_in_a.astype(_BF16), w_out_a.astype(_BF16)
    w_kv16, w_in_b16, w_out_b16 = w_kv.astype(_BF16), w_in_b.astype(_BF16), w_out_b.astype(_BF16)
    pos_s = jnp.tile(past + jnp.arange(t_dec), db)
    tab_p = _rope_tables(jnp.arange(seq), hd)
    tab_s = _rope_tables(pos_s, hd)
    tab_pq = _rope_tables(jnp.arange(seq), hd, hd ** -0.5 * LOG2_E)
    cache_k3 = cache_k.reshape(n_phys, page * n_heads * 2, hd)
    cache_v3 = cache_v.reshape(n_phys, page * n_heads, 2 * hd)

    xp = x_prompt.reshape(bsz * seq, d)
    xs = x_sample.reshape(db * t_dec, d)
    tiles_per_seq = max(seq // tm, 1)
    conv_p, conv_s = [], []
    for layer in range(n_a):
        xp, up = _conv_layer(xp, norm_a[layer], w_in_a16, conv_w, w_out_a16, layer=layer,
                             seq_rows=seq, tm=tm, tc=512)
        c_dim = up.shape[-1]
        conv_p.append(up.reshape(bsz, tiles_per_seq, SUBLANES, c_dim)[:, -1, SUBLANES - 2:, :])
        prev = state_conv[layer]
        zero = jnp.zeros_like(prev[:, :1])
        p1 = jnp.concatenate([prev[:, 1:2]] + [zero] * (t_dec - 1), axis=1)
        p2 = jnp.concatenate([prev[:, 0:1], prev[:, 1:2]] + [zero] * (t_dec - 2), axis=1)
        xs, us = _conv_layer(xs, norm_a[layer], w_in_a16, conv_w, w_out_a16, layer=layer,
                             seq_rows=t_dec, prev=(p1.reshape(db * t_dec, c_dim), p2.reshape(db * t_dec, c_dim)),
                             tm=tm, tc=512)
        conv_s.append(us.reshape(db, t_dec, c_dim)[:, t_dec - 2:, :])

    proj = functools.partial(_proj, n=width, tm=tm)
    w_kv16 = w_kv16[None]
    kp32, kp16 = proj(xp, norm_kv, w_kv16, layer=0, col_block=0, out_kinds=("f32_rows", "bf16"), name="k_prompt",
                      head_gain=k_norm, tables=tab_p, seq_rows=seq)
    vp32, vp16 = proj(xp, norm_kv, w_kv16, layer=0, col_block=1, out_kinds=("f32", "bf16"), name="v_prompt")
    ks32, = proj(xs, norm_kv, w_kv16, layer=0, col_block=0, out_kinds=("f32",), name="k_sample",
                 head_gain=k_norm, tables=tab_s, seq_rows=db * t_dec)
    vs32, = proj(xs, norm_kv, w_kv16, layer=0, col_block=1, out_kinds=("f32",), name="v_sample")
    k_new = _pad_rows(ks32.reshape(db, t_dec, width), tp)
    v_new = _pad_rows(vs32.reshape(db, t_dec, width), tp)

    for j in range(n_b):
        lam_init = 0.8 - 0.6 * math.exp(-0.3 * (n_a + j))
        q16, = proj(xp, norm_b[j], w_in_b16, layer=j, col_block=0, out_kinds=("bf16",), name="q_prompt",
                    head_gain=q_norm[j], tables=tab_pq, seq_rows=seq)
        z16, = proj(xp, norm_b[j], w_in_b16, layer=j, col_block=1, out_kinds=("bf16",), name="z_prompt")
        gp = _attn_prompt(q16.reshape(bsz, seq, width), kp16.reshape(bsz, seq, width),
                          vp16.reshape(bsz, seq, width), z16.reshape(bsz, seq, width),
                          lam[j], subln_w[j], lam_init, hd=hd, tq=512)
        xp = _out_proj(xp, gp.reshape(bsz * seq, width), w_out_b16, layer=j, tm=tm)

        qs32, = proj(xs, norm_b[j], w_in_b16, layer=j, col_block=0, out_kinds=("f32",), name="q_sample",
                     head_gain=q_norm[j], tables=tab_s, seq_rows=db * t_dec)
        zs32, = proj(xs, norm_b[j], w_in_b16, layer=j, col_block=1, out_kinds=("f32",), name="z_sample")
        qd = _pad_rows(qs32.reshape(db, t_dec, width), tp).reshape(db, tp, 2 * n_heads, hd)
        qd = qd.transpose(0, 2, 1, 3).reshape(db, 2 * n_heads * tp, hd)
        zd = _pad_rows(zs32.reshape(db, t_dec, width), tp)
        if j == 0:
            gs, k_past16, v_past16 = _attn_decode_paged(qd, cache_k3, cache_v3, page_table, k_new, v_new, zd,
                                                        lam[j], subln_w[j], lam_init, t_real=t_dec,
                                                        pages_per_step=4)
        else:
            gs = _attn_decode_dense(qd, k_past16, v_past16, k_new, v_new, zd, lam[j], subln_w[j], lam_init,
                                    t_real=t_dec, tk=2048)
        xs = _out_proj(xs, gs[:, :t_dec].reshape(db * t_dec, width), w_out_b16, layer=j, tm=tm)

    return (xp.reshape(bsz, seq, d), xs.reshape(db, t_dec, d),
            kp32.reshape(bsz, seq, n_heads, 2, hd), vp32.reshape(bsz, seq, n_heads, 2 * hd),
            jnp.stack(conv_p),
            ks32.reshape(db, t_dec, n_heads, 2, hd), vs32.reshape(db, t_dec, n_heads, 2 * hd),
            jnp.stack(conv_s))
```

```python
import functools
import math

import jax
import jax.numpy as jnp
from jax import lax
from jax.experimental import pallas as pl
from jax.experimental.pallas import tpu as pltpu

EPS = 1e-6
ROPE_THETA = 500000.0
LANES = 128
SUBLANES = 8
MXU_WIDTH = 256
VMEM_LIMIT_BYTES = 56 * 1024 * 1024
LOG2_E = math.log2(math.e)

_F32 = jnp.float32
_BF16 = jnp.bfloat16


def _dot(a, b):
    return jnp.dot(a, b, preferred_element_type=_F32)


def _dot_nt(a, b):
    return lax.dot_general(a, b, (((1,), (1,)), ((), ())), preferred_element_type=_F32)


def _silu(z):
    return z * (1.0 / (1.0 + jnp.exp(-z)))


def _rms_scale(x):
    return lax.rsqrt(jnp.mean(x * x, axis=-1, keepdims=True) + EPS)


def _params(*sem):
    return pltpu.CompilerParams(dimension_semantics=sem, vmem_limit_bytes=VMEM_LIMIT_BYTES)


def _conv_layer_kernel(*refs, prompt, tiles_per_seq, seq_rows):
    if prompt:
        (x_ref, g_ref, wb_ref, wc_ref, wh_ref, wz_ref, cw_ref, wo_ref,
         y_ref, u_ref, xn_scr, carry_scr) = refs
    else:
        (x_ref, g_ref, wb_ref, wc_ref, wh_ref, wz_ref, cw_ref, wo_ref, p1_ref, p2_ref,
         y_ref, u_ref, xn_scr) = refs
    i = pl.program_id(0)
    j = pl.program_id(1)

    @pl.when(j == 0)
    def _():
        x = x_ref[...]
        xn_scr[...] = (x * _rms_scale(x) * g_ref[...]).astype(_BF16)
        y_ref[...] = x

    xn = xn_scr[...]
    tm = xn.shape[0]
    gated = []
    for sub in range(wb_ref.shape[1] // MXU_WIDTH):
        cols = slice(sub * MXU_WIDTH, (sub + 1) * MXU_WIDTH)
        b = _dot(xn, wb_ref[:, cols])
        c = _dot(xn, wc_ref[:, cols])
        h = _dot(xn, wh_ref[:, cols])
        z = _dot(xn, wz_ref[:, cols])
        u = c * h
        row = lax.broadcasted_iota(jnp.int32, u.shape, 0)
        if prompt:
            first = (i % tiles_per_seq) == 0
            cr = carry_scr[j, :, cols]
            c0 = jnp.where(first, 0.0, cr[SUBLANES - 2:SUBLANES - 1, :])
            c1 = jnp.where(first, 0.0, cr[SUBLANES - 1:SUBLANES, :])
            um1 = jnp.where(row == 0, c1, pltpu.roll(u, 1, 0))
            um2 = jnp.where(row == 0, c0, jnp.where(row == 1, c1, pltpu.roll(u, 2, 0)))
            tail = u[tm - SUBLANES:tm, :]
            carry_scr[j, :, cols] = tail
            u_ref[0, :, cols] = tail
        else:
            t = row % seq_rows
            um1 = jnp.where(t == 0, p1_ref[:, cols], pltpu.roll(u, 1, 0))
            um2 = jnp.where(t <= 1, p2_ref[:, cols], pltpu.roll(u, 2, 0))
            u_ref[:, cols] = u
        conv = cw_ref[0:1, cols] * um2 + cw_ref[1:2, cols] * um1 + cw_ref[2:3, cols] * u
        gated.append((b * conv * _silu(z)).astype(_BF16))
    y_ref[...] += _dot(jnp.concatenate(gated, axis=1), wo_ref[...])


def _conv_layer(x, norm_g, w_in, conv_w, w_out, *, layer, seq_rows, prev=None, tm, tc):
    m, d = x.shape
    c = conv_w.shape[2]
    prompt = prev is None
    tm = min(tm, m)
    tc = min(tc, c)
    ni, nj = m // tm, c // tc
    tiles_per_seq = max(seq_rows // tm, 1)

    def w_in_spec(gate):
        return pl.BlockSpec((None, d, tc), lambda i, j, gate=gate: (layer, 0, gate * nj + j))

    in_specs = [
        pl.BlockSpec((tm, d), lambda i, j: (i, 0)),
        pl.BlockSpec((1, d), lambda i, j: (0, 0)),
        w_in_spec(0), w_in_spec(1), w_in_spec(2), w_in_spec(3),
        pl.BlockSpec((None, 3, tc), lambda i, j: (layer, 0, j)),
        pl.BlockSpec((None, tc, d), lambda i, j: (layer, j, 0)),
    ]
    args = [x, norm_g.reshape(1, d), w_in, w_in, w_in, w_in, conv_w, w_out]
    scratch = [pltpu.VMEM((tm, d), _BF16)]
    if prompt:
        u_shape = jax.ShapeDtypeStruct((ni, SUBLANES, c), _F32)
        u_spec = pl.BlockSpec((1, SUBLANES, tc), lambda i, j: (i, 0, j))
        scratch.append(pltpu.VMEM((nj, SUBLANES, tc), _F32))
    else:
        in_specs += [pl.BlockSpec((tm, tc), lambda i, j: (i, j))] * 2
        args += list(prev)
        u_shape = jax.ShapeDtypeStruct((m, c), _F32)
        u_spec = pl.BlockSpec((tm, tc), lambda i, j: (i, j))
    return pl.pallas_call(
        functools.partial(_conv_layer_kernel, prompt=prompt, tiles_per_seq=tiles_per_seq,
                          seq_rows=seq_rows),
        grid=(ni, nj),
        in_specs=in_specs,
        out_specs=[pl.BlockSpec((tm, d), lambda i, j: (i, 0)), u_spec],
        out_shape=[jax.ShapeDtypeStruct((m, d), _F32), u_shape],
        scratch_shapes=scratch,
        compiler_params=_params("arbitrary", "arbitrary"),
        name="conv_layer_prompt" if prompt else "conv_layer_sample",
    )(*args)


def _proj_heads_kernel(x_ref, g_ref, w_ref, hg_ref, cos_ref, sa_ref, sb_ref, *refs, out_kinds):
    outs = dict(zip(out_kinds, refs[:-1]))
    h_scr = refs[-1]
    i = pl.program_id(0)
    tm = x_ref.shape[0]

    @pl.when(i == 0)
    def _():
        h_scr[...] = jnp.zeros(h_scr.shape, _F32)

    cos, sa, sb, hg = cos_ref[...], sa_ref[...], sb_ref[...], hg_ref[...]
    n_groups = w_ref.shape[1] // LANES
    prev = (i + 1) % 2
    for grp in range(n_groups):
        sl = slice(grp * LANES, (grp + 1) * LANES)
        xg = h_scr[prev, :, sl]
        yn = xg * _rms_scale(xg) * hg
        y = yn * cos + pltpu.roll(yn, LANES - 16, 1) * sa + pltpu.roll(yn, 16, 1) * sb
        if "f32" in outs:
            outs["f32"][:, sl] = y
        if "f32_rows" in outs:
            outs["f32_rows"][pl.ds(grp, tm, stride=n_groups), :] = y
        if "bf16" in outs:
            outs["bf16"][:, sl] = y.astype(_BF16)

    x = x_ref[...]
    xn = (x * _rms_scale(x) * g_ref[...]).astype(_BF16)
    h_scr[i % 2] = _dot(xn, w_ref[...])


def _proj_plain_kernel(x_ref, g_ref, w_ref, *out_refs, out_kinds):
    outs = dict(zip(out_kinds, out_refs))
    x = x_ref[...]
    xn = (x * _rms_scale(x) * g_ref[...]).astype(_BF16)
    h = _dot(xn, w_ref[...])
    if "f32" in outs:
        outs["f32"][...] = h
    if "bf16" in outs:
        outs["bf16"][...] = h.astype(_BF16)


def _proj(x, norm_g, w, *, layer, n, col_block, out_kinds, tm, name, head_gain=None, tables=None,
          seq_rows=None):
    m, d = x.shape
    tm = min(tm, m)
    n_tiles = m // tm
    heads = head_gain is not None
    n_steps = n_tiles + 1 if heads else n_tiles
    in_tile = (lambda i: jnp.minimum(i, n_tiles - 1)) if heads else (lambda i: i)
    out_tile = (lambda i: jnp.maximum(i - 1, 0)) if heads else (lambda i: i)
    row_spec = pl.BlockSpec((tm, n), lambda i: (out_tile(i), 0))
    in_specs = [pl.BlockSpec((tm, d), lambda i: (in_tile(i), 0)),
                pl.BlockSpec((1, d), lambda i: (0, 0)),
                pl.BlockSpec((None, d, n), lambda i: (layer, 0, col_block))]
    args = [x, norm_g.reshape(1, d), w]
    scratch = []
    if heads:
        body = _proj_heads_kernel
        tiles_per_seq = max(seq_rows // tm, 1)
        tab_spec = pl.BlockSpec((tm, LANES), lambda i: (out_tile(i) % tiles_per_seq, 0))
        in_specs += [pl.BlockSpec((1, LANES), lambda i: (0, 0)), tab_spec, tab_spec, tab_spec]
        args += [head_gain.reshape(1, LANES), *tables]
        scratch = [pltpu.VMEM((2, tm, n), _F32)]
    else:
        body = _proj_plain_kernel
    groups = n // LANES
    out_specs = [pl.BlockSpec((tm * groups, LANES), lambda i: (out_tile(i), 0)) if kind == "f32_rows"
                 else row_spec for kind in out_kinds]
    out_shape = [jax.ShapeDtypeStruct((m * groups, LANES) if kind == "f32_rows" else (m, n),
                                      _BF16 if kind == "bf16" else _F32) for kind in out_kinds]
    return pl.pallas_call(
        functools.partial(body, out_kinds=tuple(out_kinds)),
        grid=(n_steps,),
        in_specs=in_specs,
        out_specs=out_specs,
        out_shape=out_shape,
        scratch_shapes=scratch,
        compiler_params=_params("arbitrary"),
        name=name,
    )(*args)


def _lambda_full(lam_ref, lam_init):
    lf = lam_ref[...]
    t1 = jnp.sum(lf[0:1, :] * lf[1:2, :], axis=-1, keepdims=True)
    t2 = jnp.sum(lf[2:3, :] * lf[3:4, :], axis=-1, keepdims=True)
    return jnp.exp(t1) - jnp.exp(t2) + lam_init


def _head_output(o1, o2, lam_full, z, subln, lam_init):
    o = o1 - lam_full * o2
    on = o * _rms_scale(o) * subln * (1.0 - lam_init)
    return on * _silu(z)


def _lane_repeat(x, n):
    return jnp.concatenate([x] * n, axis=1)


def _attn_prompt_kernel(lam_ref, q_ref, k_ref, v_ref, z_ref, sg_ref, o_ref,
                        kt_scr, s_scr, m_scr, l_scr, acc_scr, *, tq, hd, lam_init):
    qi = pl.program_id(2)
    n_kt = kt_scr.shape[0]
    vd = 2 * hd

    @pl.when(qi == 0)
    def _():
        for t in range(n_kt):
            kt_scr[t] = k_ref[0, t * tq:(t + 1) * tq, :].T

    q = q_ref[0]
    m_scr[...] = jnp.full(m_scr.shape, -jnp.inf, _F32)
    l_scr[...] = jnp.zeros(l_scr.shape, _F32)
    acc_scr[...] = jnp.zeros(acc_scr.shape, _F32)

    def scores(kt, slot):
        for c in range(2):
            s_scr[slot, c] = _dot(q[:, c * hd:(c + 1) * hd], kt_scr[kt, c * hd:(c + 1) * hd, :])

    def consume(kt, slot, masked):
        start = pl.multiple_of(kt * tq, tq)
        vs = v_ref[0, pl.ds(start, tq), :]
        for c in range(2):
            s = s_scr[slot, c]
            if masked:
                row = lax.broadcasted_iota(jnp.int32, s.shape, 0)
                col = lax.broadcasted_iota(jnp.int32, s.shape, 1)
                s = jnp.where(col <= row, s, -jnp.inf)
            m_prev = m_scr[c]
            m_new = jnp.maximum(m_prev, jnp.max(s, axis=-1, keepdims=True))
            alpha = jnp.exp2(m_prev - m_new)
            p = jnp.exp2(s - _lane_repeat(m_new, tq // LANES))
            l_scr[c] = alpha * l_scr[c] + jnp.sum(p, axis=-1, keepdims=True)
            acc_scr[c] = _lane_repeat(alpha, vd // LANES) * acc_scr[c] + _dot(p.astype(_BF16), vs)
            m_scr[c] = m_new

    scores(0, 0)

    def pair(i, carry):
        kt = 2 * i
        scores(kt + 1, 1)
        consume(kt, 0, False)
        scores(kt + 2, 0)
        consume(kt + 1, 1, False)
        return carry

    lax.fori_loop(0, qi // 2, pair, 0)

    @pl.when(qi % 2 == 0)
    def _():
        consume(qi, 0, True)

    @pl.when(qi % 2 == 1)
    def _():
        scores(qi, 1)
        consume(qi - 1, 0, False)
        consume(qi, 1, True)

    o1 = acc_scr[0] * _lane_repeat(1.0 / l_scr[0], vd // LANES)
    o2 = acc_scr[1] * _lane_repeat(1.0 / l_scr[1], vd // LANES)
    out = _head_output(o1, o2, _lambda_full(lam_ref, lam_init), z_ref[0].astype(_F32),
                       sg_ref[...], lam_init)
    o_ref[0] = out.astype(o_ref.dtype)


def _attn_prompt(q, k, v, z, lam, subln, lam_init, *, hd, tq):
    b, s, width = q.shape
    vd = 2 * hd
    n_heads = width // vd
    tq = min(tq, s)
    q_spec = pl.BlockSpec((1, tq, vd), lambda bi, h, qi: (bi, qi, h))
    kv_spec = pl.BlockSpec((1, s, vd), lambda bi, h, qi: (bi, 0, h))
    return pl.pallas_call(
        functools.partial(_attn_prompt_kernel, tq=tq, hd=hd, lam_init=lam_init),
        grid=(b, n_heads, s // tq),
        in_specs=[
            pl.BlockSpec((4, hd), lambda bi, h, qi: (0, 0)),
            q_spec, kv_spec, kv_spec, q_spec,
            pl.BlockSpec((1, vd), lambda bi, h, qi: (0, 0)),
        ],
        out_specs=q_spec,
        out_shape=jax.ShapeDtypeStruct((b, s, width), _BF16),
        scratch_shapes=[pltpu.VMEM((s // tq, vd, tq), _BF16), pltpu.VMEM((2, 2, tq, tq), _F32),
                        pltpu.VMEM((2, tq, LANES), _F32), pltpu.VMEM((2, tq, LANES), _F32),
                        pltpu.VMEM((2, tq, vd), _F32)],
        compiler_params=_params("arbitrary", "arbitrary", "arbitrary"),
        name="attn_prompt",
    )(lam, q, k, v, z, subln.reshape(1, vd))


def _decode_steps(cached_k, cached_v, lam_ref, q_ref, kn_ref, vn_ref, z_ref, sg_ref, o_ref,
                  m_scr, l_scr, acc_scr, *, n_heads, t_real, tp, scale, lam_init):
    hd = q_ref.shape[2]
    vd = 2 * hd
    step = pl.program_id(1)

    @pl.when(step == 0)
    def _():
        m_scr[...] = jnp.full(m_scr.shape, -jnp.inf, _F32)
        l_scr[...] = jnp.zeros(l_scr.shape, _F32)
        acc_scr[...] = jnp.zeros(acc_scr.shape, _F32)

    q = q_ref[0].astype(_BF16)

    def update(k_of_group, v_of_head, mask=None):
        s = jnp.concatenate([_dot_nt(q[g * tp:(g + 1) * tp, :], k_of_group(g)) for g in range(2 * n_heads)],
                            axis=0)
        if mask is not None:
            s = jnp.where(mask(s.shape), s, -jnp.inf)
        m_prev = m_scr[...]
        m_new = jnp.maximum(m_prev, jnp.max(s, axis=-1, keepdims=True))
        alpha = jnp.exp((m_prev - m_new) * scale)
        p = jnp.exp((s - m_new) * scale)
        l_scr[...] = alpha * l_scr[...] + jnp.sum(p, axis=-1, keepdims=True)
        p = p.astype(_BF16)
        pv = jnp.concatenate([_dot(p[h * 2 * tp:(h + 1) * 2 * tp, :], v_of_head(h)) for h in range(n_heads)],
                             axis=0)
        acc_scr[...] = alpha * acc_scr[...] + pv
        m_scr[...] = m_new

    update(cached_k, cached_v)

    @pl.when(step == pl.num_programs(1) - 1)
    def _():
        def new_key_mask(shape):
            t = lax.broadcasted_iota(jnp.int32, shape, 0) % tp
            jj = lax.broadcasted_iota(jnp.int32, shape, 1)
            return (jj <= t) & (jj < t_real)

        update(lambda g: kn_ref[0, :, g * hd:(g + 1) * hd].astype(_BF16),
               lambda h: vn_ref[0, :, h * vd:(h + 1) * vd].astype(_BF16), new_key_mask)
        lam_full = _lambda_full(lam_ref, lam_init)
        acc = acc_scr[...] * (1.0 / l_scr[...])
        for h in range(n_heads):
            r1 = (2 * h) * tp
            cols = slice(h * vd, (h + 1) * vd)
            out = _head_output(acc[r1:r1 + tp, :], acc[r1 + tp:r1 + 2 * tp, :], lam_full,
                               z_ref[0, :, cols], sg_ref[...], lam_init)
            o_ref[0, :, cols] = out.astype(o_ref.dtype)


PAGE_RING_SLOTS = 3


def _ring_fetch(copies, lin, total):
    @pl.when(lin == 0)
    def _():
        for ahead in range(PAGE_RING_SLOTS - 1):
            @pl.when(ahead < total)
            def _():
                for cp in copies(ahead, ahead):
                    cp.start()

    @pl.when(lin + PAGE_RING_SLOTS - 1 < total)
    def _():
        nxt = lin + PAGE_RING_SLOTS - 1
        for cp in copies(nxt, nxt % PAGE_RING_SLOTS):
            cp.start()

    for cp in copies(lin, lin % PAGE_RING_SLOTS):
        cp.wait()


def _page_copies(pt_ref, k_hbm, v_hbm, kbuf, vbuf, sem, step, slot, *, pages_per_step):
    v_blocks = v_hbm.shape[2] // LANES
    copies = []
    for g in range(pages_per_step):
        pid = pt_ref[step * pages_per_step + g]
        i = slot * pages_per_step + g
        copies.append(pltpu.make_async_copy(k_hbm.at[pid], kbuf.at[i], sem.at[slot]))
        for cb in range(v_blocks):
            copies.append(pltpu.make_async_copy(v_hbm.at[pid, :, pl.ds(cb * LANES, LANES)],
                                                vbuf.at[i * v_blocks + cb], sem.at[slot]))
    return copies


def _attn_decode_paged_kernel(pt_ref, lam_ref, q_ref, kn_ref, vn_ref, z_ref, sg_ref, k_hbm, v_hbm,
                              o_ref, kd_ref, vd_ref, kbuf, vbuf, sem, m_scr, l_scr, acc_scr, *,
                              pages_per_step, n_heads, **kw):
    hd = q_ref.shape[2]
    vd = 2 * hd
    v_blocks = vd // LANES
    k_rows = 2 * n_heads
    page = k_hbm.shape[1] // k_rows
    n_steps = pl.num_programs(1)
    total = pl.num_programs(0) * n_steps
    lin = pl.program_id(0) * n_steps + pl.program_id(1)
    slot = lin % PAGE_RING_SLOTS
    _ring_fetch(functools.partial(_page_copies, pt_ref, k_hbm, v_hbm, kbuf, vbuf, sem,
                                  pages_per_step=pages_per_step), lin, total)

    def cached_k(g):
        k = jnp.concatenate([kbuf[slot * pages_per_step + pg, pl.ds(g, page, stride=k_rows), :]
                             for pg in range(pages_per_step)], axis=0).astype(_BF16)
        kd_ref[0, :, g * hd:(g + 1) * hd] = k
        return k

    def cached_v(h):
        v = jnp.concatenate(
            [jnp.concatenate([vbuf[(slot * pages_per_step + pg) * v_blocks + cb,
                                   pl.ds(h, page, stride=n_heads), :] for cb in range(v_blocks)], axis=1)
             for pg in range(pages_per_step)], axis=0).astype(_BF16)
        vd_ref[0, :, h * vd:(h + 1) * vd] = v
        return v

    _decode_steps(cached_k, cached_v, lam_ref, q_ref, kn_ref, vn_ref, z_ref, sg_ref, o_ref,
                  m_scr, l_scr, acc_scr, n_heads=n_heads, **kw)


def _dense_copies(k_hbm, v_hbm, kbuf, vbuf, sem, step, slot, *, n_steps):
    tk = kbuf.shape[1]
    rows = pl.ds(pl.multiple_of((step % n_steps) * tk, tk), tk)
    return [pltpu.make_async_copy(hbm.at[step // n_steps, rows], buf.at[slot], sem.at[slot])
            for hbm, buf in ((k_hbm, kbuf), (v_hbm, vbuf))]


def _attn_decode_dense_kernel(lam_ref, q_ref, kn_ref, vn_ref, z_ref, sg_ref, k_hbm, v_hbm, o_ref,
                              kbuf, vbuf, sem, m_scr, l_scr, acc_scr, **kw):
    hd = q_ref.shape[2]
    vd = 2 * hd
    n_steps = pl.num_programs(1)
    lin = pl.program_id(0) * n_steps + pl.program_id(1)
    slot = lin % PAGE_RING_SLOTS
    _ring_fetch(functools.partial(_dense_copies, k_hbm, v_hbm, kbuf, vbuf, sem, n_steps=n_steps),
                lin, pl.num_programs(0) * n_steps)
    _decode_steps(lambda g: kbuf[slot, :, g * hd:(g + 1) * hd], lambda h: vbuf[slot, :, h * vd:(h + 1) * vd],
                  lam_ref, q_ref, kn_ref, vn_ref, z_ref, sg_ref, o_ref, m_scr, l_scr, acc_scr, **kw)


def _decode_scratch(rows, vd):
    return [pltpu.VMEM((rows, 1), _F32), pltpu.VMEM((rows, 1), _F32), pltpu.VMEM((rows, vd), _F32)]


def _attn_decode_paged(q, cache_k, cache_v, page_table, k_new, v_new, z, lam, subln, lam_init, *,
                       t_real, pages_per_step):
    db, rows, hd = q.shape
    vd = 2 * hd
    v_blocks = vd // LANES
    width = k_new.shape[2]
    n_heads = width // vd
    n_pages = page_table.shape[1]
    page = cache_v.shape[1] // n_heads
    tp = rows // (2 * n_heads)
    g = math.gcd(pages_per_step, n_pages)

    row_spec = pl.BlockSpec((1, tp, width), lambda b, s, pt: (b, 0, 0))
    dense_spec = pl.BlockSpec((1, g * page, width), lambda b, s, pt: (b, s, 0))
    dense_shape = jax.ShapeDtypeStruct((db, n_pages * page, width), _BF16)
    grid_spec = pltpu.PrefetchScalarGridSpec(
        num_scalar_prefetch=1,
        grid=(db, n_pages // g),
        in_specs=[
            pl.BlockSpec((4, hd), lambda b, s, pt: (0, 0)),
            pl.BlockSpec((1, rows, hd), lambda b, s, pt: (b, 0, 0)),
            row_spec, row_spec, row_spec,
            pl.BlockSpec((1, vd), lambda b, s, pt: (0, 0)),
            pl.BlockSpec(memory_space=pl.ANY),
            pl.BlockSpec(memory_space=pl.ANY),
        ],
        out_specs=[row_spec, dense_spec, dense_spec],
        scratch_shapes=[
            pltpu.VMEM((PAGE_RING_SLOTS * g, cache_k.shape[1], LANES), _F32),
            pltpu.VMEM((PAGE_RING_SLOTS * g * v_blocks, cache_v.shape[1], LANES), _F32),
            pltpu.SemaphoreType.DMA((PAGE_RING_SLOTS,)),
            *_decode_scratch(rows, vd),
        ],
    )
    return pl.pallas_call(
        functools.partial(_attn_decode_paged_kernel, pages_per_step=g, n_heads=n_heads, t_real=t_real,
                          tp=tp, scale=hd ** -0.5, lam_init=lam_init),
        grid_spec=grid_spec,
        out_shape=[jax.ShapeDtypeStruct((db, tp, width), _BF16), dense_shape, dense_shape],
        compiler_params=_params("arbitrary", "arbitrary"),
        name="attn_decode_paged",
    )(page_table.reshape(-1), lam, q, k_new, v_new, z, subln.reshape(1, vd), cache_k, cache_v)


def _attn_decode_dense(q, k_past, v_past, k_new, v_new, z, lam, subln, lam_init, *, t_real, tk):
    db, rows, hd = q.shape
    vd = 2 * hd
    past, width = k_past.shape[1:]
    n_heads = width // vd
    tp = rows // (2 * n_heads)
    tk = math.gcd(tk, past)
    row_spec = pl.BlockSpec((1, tp, width), lambda b, s: (b, 0, 0))
    return pl.pallas_call(
        functools.partial(_attn_decode_dense_kernel, n_heads=n_heads, t_real=t_real, tp=tp,
                          scale=hd ** -0.5, lam_init=lam_init),
        grid=(db, past // tk),
        in_specs=[
            pl.BlockSpec((4, hd), lambda b, s: (0, 0)),
            pl.BlockSpec((1, rows, hd), lambda b, s: (b, 0, 0)),
            row_spec, row_spec, row_spec,
            pl.BlockSpec((1, vd), lambda b, s: (0, 0)),
            pl.BlockSpec(memory_space=pl.ANY),
            pl.BlockSpec(memory_space=pl.ANY),
        ],
        out_specs=row_spec,
        out_shape=jax.ShapeDtypeStruct((db, tp, width), _BF16),
        scratch_shapes=[pltpu.VMEM((PAGE_RING_SLOTS, tk, width), _BF16),
                        pltpu.VMEM((PAGE_RING_SLOTS, tk, width), _BF16),
                        pltpu.SemaphoreType.DMA((PAGE_RING_SLOTS,)),
                        *_decode_scratch(rows, vd)],
        compiler_params=_params("arbitrary", "arbitrary"),
        name="attn_decode_dense",
    )(lam, q, k_new, v_new, z, subln.reshape(1, vd), k_past, v_past)


def _out_proj_kernel(x_ref, g_ref, w_ref, y_ref):
    y_ref[...] = x_ref[...] + _dot(g_ref[...], w_ref[...])


def _out_proj(x, g, w, *, layer, tm):
    m, d = x.shape
    a = g.shape[1]
    tm = min(tm, m)
    return pl.pallas_call(
        _out_proj_kernel,
        grid=(m // tm,),
        in_specs=[pl.BlockSpec((tm, d), lambda i: (i, 0)),
                  pl.BlockSpec((tm, a), lambda i: (i, 0)),
                  pl.BlockSpec((None, a, d), lambda i: (layer, 0, 0))],
        out_specs=pl.BlockSpec((tm, d), lambda i: (i, 0)),
        out_shape=jax.ShapeDtypeStruct((m, d), _F32),
        compiler_params=_params("arbitrary"),
        name="out_proj",
    )(x, g, w)


def _rope_tables(pos, hd, out_scale=1.0):
    rot = hd // 4
    half = rot // 2
    inv = ROPE_THETA ** (-jnp.arange(0, rot, 2, dtype=_F32) / rot)
    ang = pos.astype(_F32)[:, None] * inv[None, :]
    cos, sin = jnp.cos(ang), jnp.sin(ang)
    n = pos.shape[0]
    cos_t = jnp.concatenate([cos, cos, jnp.ones((n, hd - rot), _F32)], axis=1)
    sa_t = jnp.concatenate([-sin, jnp.zeros((n, hd - half), _F32)], axis=1)
    sb_t = jnp.concatenate([jnp.zeros((n, half), _F32), sin, jnp.zeros((n, hd - rot), _F32)], axis=1)
    return cos_t * out_scale, sa_t * out_scale, sb_t * out_scale


def _pad_rows(x, tp):
    return jnp.pad(x, ((0, 0), (0, tp - x.shape[1]), (0, 0)))


def kernel(x_prompt, x_sample, state_conv, cache_k, cache_v, page_table, norm_a, w_in_a, conv_w, w_out_a, norm_kv, w_kv, k_norm, norm_b, w_in_b, q_norm, lam, subln_w, w_out_b):
    bsz, seq, d = x_prompt.shape
    db, t_dec, _ = x_sample.shape
    n_a, n_b = norm_a.shape[0], norm_b.shape[0]
    n_phys, page, n_heads, _, hd = cache_k.shape
    n_pages = page_table.shape[1]
    past = n_pages * page
    width = n_heads * 2 * hd
    tp = -(-t_dec // SUBLANES) * SUBLANES
    tm = 512

    w_in_a16, w_out_a16 = w_in_a.astype(_BF16), w_out_a.astype(_BF16)
    w_kv16, w_in_b16, w_out_b16 = w_kv.astype(_BF16), w_in_b.astype(_BF16), w_out_b.astype(_BF16)
    pos_s = jnp.tile(past + jnp.arange(t_dec), db)
    tab_p = _rope_tables(jnp.arange(seq), hd)
    tab_s = _rope_tables(pos_s, hd)
    tab_pq = _rope_tables(jnp.arange(seq), hd, hd ** -0.5 * LOG2_E)
    cache_k3 = cache_k.reshape(n_phys, page * n_heads * 2, hd)
    cache_v3 = cache_v.reshape(n_phys, page * n_heads, 2 * hd)

    xp = x_prompt.reshape(bsz * seq, d)
    xs = x_sample.reshape(db * t_dec, d)
    tiles_per_seq = max(seq // tm, 1)
    conv_p, conv_s = [], []
    for layer in range(n_a):
        xp, up = _conv_layer(xp, norm_a[layer], w_in_a16, conv_w, w_out_a16, layer=layer,
                             seq_rows=seq, tm=tm, tc=512)
        c_dim = up.shape[-1]
        conv_p.append(up.reshape(bsz, tiles_per_seq, SUBLANES, c_dim)[:, -1, SUBLANES - 2:, :])
        prev = state_conv[layer]
        zero = jnp.zeros_like(prev[:, :1])
        p1 = jnp.concatenate([prev[:, 1:2]] + [zero] * (t_dec - 1), axis=1)
        p2 = jnp.concatenate([prev[:, 0:1], prev[:, 1:2]] + [zero] * (t_dec - 2), axis=1)
        xs, us = _conv_layer(xs, norm_a[layer], w_in_a16, conv_w, w_out_a16, layer=layer,
                             seq_rows=t_dec, prev=(p1.reshape(db * t_dec, c_dim), p2.reshape(db * t_dec, c_dim)),
                             tm=tm, tc=512)
        conv_s.append(us.reshape(db, t_dec, c_dim)[:, t_dec - 2:, :])

    proj = functools.partial(_proj, n=width, tm=tm)
    w_kv16 = w_kv16[None]
    kp32, kp16 = proj(xp, norm_kv, w_kv16, layer=0, col_block=0, out_kinds=("f32_rows", "bf16"), name="k_prompt",
                      head_gain=k_norm, tables=tab_p, seq_rows=seq)
    vp32, vp16 = proj(xp, norm_kv, w_kv16, layer=0, col_block=1, out_kinds=("f32", "bf16"), name="v_prompt")
    ks32, = proj(xs, norm_kv, w_kv16, layer=0, col_block=0, out_kinds=("f32",), name="k_sample",
                 head_gain=k_norm, tables=tab_s, seq_rows=db * t_dec)
    vs32, = proj(xs, norm_kv, w_kv16, layer=0, col_block=1, out_kinds=("f32",), name="v_sample")
    k_new = _pad_rows(ks32.reshape(db, t_dec, width), tp)
    v_new = _pad_rows(vs32.reshape(db, t_dec, width), tp)

    for j in range(n_b):
        lam_init = 0.8 - 0.6 * math.exp(-0.3 * (n_a + j))
        q16, = proj(xp, norm_b[j], w_in_b16, layer=j, col_block=0, out_kinds=("bf16",), name="q_prompt",
                    head_gain=q_norm[j], tables=tab_pq, seq_rows=seq)
        z16, = proj(xp, norm_b[j], w_in_b16, layer=j, col_block=1, out_kinds=("bf16",), name="z_prompt")
        gp = _attn_prompt(q16.reshape(bsz, seq, width), kp16.reshape(bsz, seq, width),
                          vp16.reshape(bsz, seq, width), z16.reshape(bsz, seq, width),
                          lam[j], subln_w[j], lam_init, hd=hd, tq=512)
        xp = _out_proj(xp, gp.reshape(bsz * seq, width), w_out_b16, layer=j, tm=tm)

        qs32, = proj(xs, norm_b[j], w_in_b16, layer=j, col_block=0, out_kinds=("f32",), name="q_sample",
                     head_gain=q_norm[j], tables=tab_s, seq_rows=db * t_dec)
        zs32, = proj(xs, norm_b[j], w_in_b16, layer=j, col_block=1, out_kinds=("f32",), name="z_sample")
        qd = _pad_rows(qs32.reshape(db, t_dec, width), tp).reshape(db, tp, 2 * n_heads, hd)
        qd = qd.transpose(0, 2, 1, 3).reshape(db, 2 * n_heads * tp, hd)
        zd = _pad_rows(zs32.reshape(db, t_dec, width), tp)
        if j == 0:
            gs, k_past16, v_past16 = _attn_decode_paged(qd, cache_k3, cache_v3, page_table, k_new, v_new, zd,
                                                        lam[j], subln_w[j], lam_init, t_real=t_dec,
                                                        pages_per_step=4)
        else:
            gs = _attn_decode_dense(qd, k_past16, v_past16, k_new, v_new, zd, lam[j], subln_w[j], lam_init,
                                    t_real=t_dec, tk=1024)
        xs = _out_proj(xs, gs[:, :t_dec].reshape(db * t_dec, width), w_out_b16, layer=j, tm=tm)

    return (xp.reshape(bsz, seq, d), xs.reshape(db, t_dec, d),
            kp32.reshape(bsz, seq, n_heads, 2, hd), vp32.reshape(bsz, seq, n_heads, 2 * hd),
            jnp.stack(conv_p),
            ks32.reshape(db, t_dec, n_heads, 2, hd), vs32.reshape(db, t_dec, n_heads, 2 * hd),
            jnp.stack(conv_s))
```
